```python
import jax, jax.numpy as jnp
from jax import lax
import numpy as np

D_MODEL = 1024
BATCH = 16
SEQ = 256
DEPTH = 4
DEC_BATCH = 4
DEC_SEQ = 1024
PAST_LEN = 256

GRID_W = 64
D_A = 256
D_B = 256
D_C = 512
D_MIX = D_A + D_B + D_C
N_HEADS_C = 4
HEAD_K = D_C // N_HEADS_C
HEAD_V = D_C // N_HEADS_C
CONV_A = 3
CONV_B = 31
CHUNK = 32
N_EXPERTS = 16
N_GROUPS = 4
EXPERTS_PER_GROUP = N_EXPERTS // N_GROUPS
TOP_K = 2
GROUP_SCORE_K = 2
D_EXPERT = 512
N_MOD = 6
EPS = 1e-6
MASK_VALUE = -1e9
IN_WIDTHS = (D_A, D_A, D_A, D_B, D_B, D_C, D_C, D_C, D_C, D_C)
D_IN = sum(IN_WIDTHS)

kernel_name = 'hybrid_conv_hgrn2_moe_prefix_flow_step'


def rms_norm(x, g):
    xf = x.astype(jnp.float32)
    y = xf * lax.rsqrt(jnp.mean(xf * xf, axis=-1, keepdims=True) + EPS)
    return (y * g.astype(jnp.float32)).astype(x.dtype)


def layer_norm(x, g, b):
    xf = x.astype(jnp.float32)
    mu = jnp.mean(xf, axis=-1, keepdims=True)
    var = jnp.mean(jnp.square(xf - mu), axis=-1, keepdims=True)
    y = (xf - mu) * lax.rsqrt(var + EPS)
    return (y * g.astype(jnp.float32) + b.astype(jnp.float32)).astype(x.dtype)


def depthwise_conv(x, w):
    k = w.shape[0]
    return lax.conv_general_dilated(
        x, w[:, None, :].astype(x.dtype), window_strides=(1,), padding=[(k // 2, k // 2)],
        dimension_numbers=('NWC', 'WIO', 'NWC'), feature_group_count=x.shape[-1])


def conv_rows(x, w, layout):
    if layout == 'context':
        return depthwise_conv(x, w)
    b, t, c = x.shape
    rows = t // GRID_W
    return depthwise_conv(x.reshape(b * rows, GRID_W, c), w).reshape(b, t, c)


def conv_cols(x, w, layout):
    if layout == 'context':
        return depthwise_conv(x, w)
    b, t, c = x.shape
    rows = t // GRID_W
    xc = x.reshape(b, rows, GRID_W, c).transpose(0, 2, 1, 3).reshape(b * GRID_W, rows, c)
    y = depthwise_conv(xc, w)
    return y.reshape(b, GRID_W, rows, c).transpose(0, 2, 1, 3).reshape(b, t, c)


def gla_chunk_scan(q, k, v, log_f, s0):
    b, t, h, _ = q.shape
    dv = v.shape[-1]
    n = t // CHUNK

    def blocks(a):
        return a.reshape(b, n, CHUNK, h, a.shape[-1]).transpose(1, 0, 3, 2, 4)

    qc, kc, vc, gc = blocks(q), blocks(k), blocks(v), blocks(log_f)
    cum = jnp.cumsum(gc, axis=3)
    last = cum[:, :, :, -1:, :]
    causal = jnp.tril(jnp.ones((CHUNK, CHUNK), dtype=bool))[:, :, None]
    diff = cum[:, :, :, :, None, :] - cum[:, :, :, None, :, :]
    decay = jnp.where(causal, jnp.exp(jnp.where(causal, diff, 0.0)), 0.0)
    scores = jnp.einsum('nbhtk,nbhsk,nbhtsk->nbhts', qc, kc, decay)
    o_intra = jnp.einsum('nbhts,nbhsv->nbhtv', scores, vc)
    q_dec = qc * jnp.exp(cum)
    k_dec = kc * jnp.exp(last - cum)
    kv = jnp.einsum('nbhsk,nbhsv->nbhkv', k_dec, vc)
    chunk_decay = jnp.exp(last[:, :, :, 0, :])

    def step(s, inp):
        dec, kv_c = inp
        return dec[..., None] * s + kv_c, s

    s_final, s_before = lax.scan(step, s0.astype(jnp.float32), (chunk_decay, kv))
    o = o_intra + jnp.einsum('nbhtk,nbhkv->nbhtv', q_dec, s_before)
    return o.transpose(1, 0, 3, 2, 4).reshape(b, t, h, dv), s_final


def hgrn2_mixer(q_pre, i_pre, ff_pre, fb_pre, g_pre, lb_fwd, lb_bwd, gn_g, s0):
    b, t, _ = q_pre.shape

    def heads(a):
        return a.astype(jnp.float32).reshape(b, t, N_HEADS_C, -1)

    q = jax.nn.silu(heads(q_pre)) * (HEAD_K ** -0.5)
    v = heads(i_pre)

    def gates(f_pre, lb):
        x = heads(f_pre)
        lbh = lb.astype(jnp.float32).reshape(N_HEADS_C, HEAD_K)
        f = lbh + (1.0 - lbh) * jax.nn.sigmoid(x)
        return (1.0 - lbh) * jax.nn.sigmoid(-x), jnp.log(f)

    k_f, lf_f = gates(ff_pre, lb_fwd)
    k_b, lf_b = gates(fb_pre, lb_bwd)
    o_f, s_f = gla_chunk_scan(q, k_f, v, lf_f, s0[:, 0])
    flip = lambda a: jnp.flip(a, axis=1)
    o_b, s_b = gla_chunk_scan(flip(q), flip(k_b), flip(v), flip(lf_b), s0[:, 1])
    o = o_f + flip(o_b)
    o = o * lax.rsqrt(jnp.mean(o * o, axis=-1, keepdims=True) + EPS)
    o = o * gn_g.astype(jnp.float32).reshape(N_HEADS_C, HEAD_V)
    o = o.reshape(b, t, D_C) * jax.nn.silu(g_pre.astype(jnp.float32))
    return o.astype(q_pre.dtype), jnp.stack([s_f, s_b], axis=1)


def token_mix(h, w_in, w_out, conv_a_w, conv_b_w, conv_b_b, ln_g, ln_b, lb_fwd, lb_bwd, gn_g, s0, layout):
    proj = h @ w_in
    splits = [int(s) for s in np.cumsum(IN_WIDTHS)[:-1]]
    a_b, a_c, a_x, b_a, b_g, c_q, c_i, c_ff, c_fb, c_g = jnp.split(proj, splits, axis=-1)
    y_a = a_b * conv_rows(a_c * a_x, conv_a_w, layout)
    u = conv_cols(b_a * jax.nn.sigmoid(b_g), conv_b_w, layout) + conv_b_b
    y_b = jax.nn.silu(layer_norm(u, ln_g, ln_b))
    y_c, s_new = hgrn2_mixer(c_q, c_i, c_ff, c_fb, c_g, lb_fwd, lb_bwd, gn_g, s0)
    y = jnp.concatenate([y_a, y_b, y_c], axis=-1) @ w_out
    return y, s_new


def routed_moe(h, w_router, b_router, w_gate, w_up, w_down):
    b, t, d = h.shape
    hf = h.reshape(b * t, d)
    scores = jax.nn.softmax((hf @ w_router).astype(jnp.float32), axis=-1)
    biased = scores + b_router.astype(jnp.float32)
    group_score = lax.top_k(biased.reshape(-1, N_GROUPS, EXPERTS_PER_GROUP), GROUP_SCORE_K)[0].sum(-1)
    best = jnp.argmax(group_score, axis=-1)
    in_group = (jnp.arange(N_EXPERTS) // EXPERTS_PER_GROUP)[None, :] == best[:, None]
    _, idx = lax.top_k(jnp.where(in_group, biased, MASK_VALUE), TOP_K)
    w_sel = jnp.take_along_axis(scores, idx, axis=-1)
    w_sel = w_sel / jnp.sum(w_sel, axis=-1, keepdims=True)
    gates = jnp.einsum('nk,nke->ne', w_sel, jax.nn.one_hot(idx, N_EXPERTS, dtype=jnp.float32))
    hg = jnp.einsum('nd,edf->nef', hf, w_gate)
    hu = jnp.einsum('nd,edf->nef', hf, w_up)
    act = jax.nn.silu(hg) * hu * gates[:, :, None].astype(h.dtype)
    return jnp.einsum('nef,efd->nd', act, w_down).reshape(b, t, d)


def run_stream(x, cond, s_init, layout, lower_bound, norm_mix_g, norm_ffn_g, final_norm_g, w_ada, b_ada,
               w_in, w_out, conv_a_w, conv_b_w, conv_b_b, ln_b_g, ln_b_b, gnorm_c_g, w_router, b_router,
               w_gate, w_up, w_down):
    cond_act = jax.nn.silu(cond)
    states = []
    for l in range(DEPTH):
        mod = (cond_act @ w_ada[l] + b_ada[l])[:, None, :]
        sh1, sc1, g1, sh2, sc2, g2 = jnp.split(mod, N_MOD, axis=-1)
        h = rms_norm(x, norm_mix_g[l]) * (1 + sc1) + sh1
        y, s_new = token_mix(h, w_in[l], w_out[l], conv_a_w[l], conv_b_w[l], conv_b_b[l], ln_b_g[l], ln_b_b[l],
                             lower_bound[0, l], lower_bound[1, l], gnorm_c_g[l], s_init[:, l], layout)
        x = x + g1 * y
        if layout == 'context':
            states.append(s_new)
        h = rms_norm(x, norm_ffn_g[l]) * (1 + sc2) + sh2
        x = x + g2 * routed_moe(h, w_router, b_router, w_gate[l], w_up[l], w_down[l])
    y = rms_norm(x, final_norm_g)
    return y, (jnp.stack(states, axis=1) if states else None)


def setup_inputs(seed: int = 0) -> dict:
    key = jax.random.key(seed)
    ks = jax.random.split(key, 26)
    nrm = lambda k, shape, s: jax.random.normal(k, shape, jnp.float32) * s
    return {
        'x_prompt': nrm(ks[0], (BATCH, SEQ, D_MODEL), 1.0),
        'x_sample': nrm(ks[1], (DEC_BATCH, DEC_SEQ, D_MODEL), 1.0),
        'state_hgrn': nrm(ks[2], (DEC_BATCH, DEPTH, 2, N_HEADS_C, HEAD_K, HEAD_V), 0.5),
        'c': nrm(ks[3], (DEC_BATCH, D_MODEL), 1.0),
        'c_ctx': nrm(ks[4], (D_MODEL,), 1.0),
        'norm_mix_g': 1.0 + nrm(ks[5], (DEPTH, D_MODEL), 0.02),
        'norm_ffn_g': 1.0 + nrm(ks[6], (DEPTH, D_MODEL), 0.02),
        'final_norm_g': 1.0 + nrm(ks[7], (D_MODEL,), 0.02),
        'w_ada': nrm(ks[8], (DEPTH, D_MODEL, N_MOD * D_MODEL), 0.5 * D_MODEL ** -0.5),
        'b_ada': nrm(ks[9], (DEPTH, N_MOD * D_MODEL), 0.02),
        'w_in': nrm(ks[10], (DEPTH, D_MODEL, D_IN), D_MODEL ** -0.5),
        'w_out': nrm(ks[11], (DEPTH, D_MIX, D_MODEL), D_MIX ** -0.5),
        'conv_a_w': nrm(ks[12], (DEPTH, CONV_A, D_A), CONV_A ** -0.5),
        'conv_b_w': nrm(ks[13], (DEPTH, CONV_B, D_B), CONV_B ** -0.5),
        'conv_b_b': nrm(ks[14], (DEPTH, D_B), 0.02),
        'ln_b_g': 1.0 + nrm(ks[15], (DEPTH, D_B), 0.02),
        'ln_b_b': nrm(ks[16], (DEPTH, D_B), 0.02),
        'lb_logits': nrm(ks[17], (2, DEPTH, D_C), 0.5),
        'gnorm_c_g': 1.0 + nrm(ks[18], (DEPTH, D_C), 0.02),
        'w_router': nrm(ks[19], (D_MODEL, N_EXPERTS), D_MODEL ** -0.5),
        'b_router': nrm(ks[20], (N_EXPERTS,), 0.01),
        'w_gate': nrm(ks[21], (DEPTH, N_EXPERTS, D_MODEL, D_EXPERT), D_MODEL ** -0.5),
        'w_up': nrm(ks[22], (DEPTH, N_EXPERTS, D_MODEL, D_EXPERT), D_MODEL ** -0.5),
        'w_down': nrm(ks[23], (DEPTH, N_EXPERTS, D_EXPERT, D_MODEL), D_EXPERT ** -0.5),
    }


def reference(x_prompt, x_sample, state_hgrn, c, c_ctx, norm_mix_g, norm_ffn_g, final_norm_g, w_ada, b_ada,
              w_in, w_out, conv_a_w, conv_b_w, conv_b_b, ln_b_g, ln_b_b, lb_logits, gnorm_c_g, w_router,
              b_router, w_gate, w_up, w_down):
    p = jax.nn.softmax(lb_logits.astype(jnp.float32), axis=1)
    lower_bound = jnp.cumsum(p, axis=1) - p[:, :1]
    s_zero = jnp.zeros((x_prompt.shape[0], DEPTH, 2, N_HEADS_C, HEAD_K, HEAD_V), jnp.float32)
    y_prompt, new_state_hgrn = run_stream(
        x_prompt, c_ctx[None, :], s_zero, 'context', lower_bound, norm_mix_g, norm_ffn_g, final_norm_g,
        w_ada, b_ada, w_in, w_out, conv_a_w, conv_b_w, conv_b_b, ln_b_g, ln_b_b, gnorm_c_g, w_router,
        b_router, w_gate, w_up, w_down)
    y_sample, _ = run_stream(
        x_sample, c, state_hgrn, 'latent', lower_bound, norm_mix_g, norm_ffn_g, final_norm_g,
        w_ada, b_ada, w_in, w_out, conv_a_w, conv_b_w, conv_b_b, ln_b_g, ln_b_b, gnorm_c_g, w_router,
        b_router, w_gate, w_up, w_down)
    return (y_prompt, y_sample, new_state_hgrn)
```

```python
import functools

import jax
import jax.numpy as jnp
from jax import lax
from jax.experimental import pallas as pl
from jax.experimental.pallas import tpu as pltpu

F32 = jnp.float32
BF16 = jnp.bfloat16

D_MODEL = 1024
DEPTH = 4
GRID_W = 64
D_A = 256
D_B = 256
D_C = 512
N_HEADS_C = 4
HEAD = 128
CONV_A = 3
CONV_B = 31
N_EXPERTS = 16
N_GROUPS = 4
EXPERTS_PER_GROUP = 4
D_EXPERT = 512
N_MOD = 6
D_IN = 3840
EPS = 1e-6
MASK_VALUE = -1e9

ROWS = 4096
TILE = 1024
CHUNK = 128
N_LEVELS = 7
VMEM_LIMIT = 52 * 1024 * 1024


def _params(*sem):
    return pltpu.CompilerParams(dimension_semantics=sem, vmem_limit_bytes=VMEM_LIMIT)


def _dot(a, b):
    return jnp.dot(a, b, preferred_element_type=F32)


def _dot_nt(a, b):
    return lax.dot_general(a, b, (((1,), (1,)), ((), ())), preferred_element_type=F32)


def _dot_tn(a, b):
    return lax.dot_general(a, b, (((0,), (0,)), ((), ())), preferred_element_type=F32)


def _sigmoid_pair(x):
    e = jnp.exp(-jnp.abs(x))
    r = 1.0 / (1.0 + e)
    er = e * r
    pos = x >= 0
    return jnp.where(pos, r, er), jnp.where(pos, er, r)


def _silu(x):
    return x * _sigmoid_pair(x)[0]


def _lower_bound_kernel(lg_ref, out_ref):
    for d in range(2):
        x = lg_ref[d]
        mx = jnp.max(x, axis=0, keepdims=True)
        e = jnp.exp(x - mx)
        p = e / jnp.sum(e, axis=0, keepdims=True)
        run = jnp.zeros_like(p[0:1])
        rows = []
        for l in range(DEPTH):
            run = run + p[l:l + 1]
            rows.append(run - p[0:1])
        out_ref[d] = jnp.concatenate(rows, axis=0)


def _lower_bound(lb_logits):
    return pl.pallas_call(
        _lower_bound_kernel,
        out_shape=jax.ShapeDtypeStruct((2, DEPTH, D_C), F32),
        name="lower_bound",
    )(lb_logits)


def _adaln_kernel(cond_ref, w_ref, b_ref, out_ref):
    a = _silu(cond_ref[...]).astype(BF16)
    out_ref[0] = _dot(a, w_ref[0].astype(BF16)) + b_ref[0]


def _adaln(cond8, w_ada, b_ada):
    tn = 1536
    nt = (N_MOD * D_MODEL) // tn
    return pl.pallas_call(
        _adaln_kernel,
        grid=(DEPTH, nt),
        in_specs=[
            pl.BlockSpec((8, D_MODEL), lambda l, j: (0, 0)),
            pl.BlockSpec((1, D_MODEL, tn), lambda l, j: (l, 0, j)),
            pl.BlockSpec((1, 1, tn), lambda l, j: (l, 0, j)),
        ],
        out_specs=pl.BlockSpec((1, 8, tn), lambda l, j: (l, 0, j)),
        out_shape=jax.ShapeDtypeStruct((DEPTH, 8, N_MOD * D_MODEL), F32),
        compiler_params=_params("arbitrary", "arbitrary"),
        name="adaln",
    )(cond8, w_ada, b_ada.reshape(DEPTH, 1, N_MOD * D_MODEL))


def _rms(x):
    return x * lax.rsqrt(jnp.mean(x * x, axis=-1, keepdims=True) + EPS)


def _inproj_kernel(x_ref, mod_ref, g_ref, w_ref, out_ref, h_scr):
    @pl.when(pl.program_id(1) == 0)
    def _():
        m = mod_ref[0]
        sh = m[:, 0:D_MODEL]
        sc = m[:, D_MODEL:2 * D_MODEL]
        h = _rms(x_ref[...]) * g_ref[...] * (1.0 + sc) + sh
        h_scr[...] = h.astype(BF16)

    out_ref[...] = _dot(h_scr[...], w_ref[...])


def _inproj(x, mod_l, norm_g, w_in_bf, cond_of_tile):
    tn = 768
    return pl.pallas_call(
        _inproj_kernel,
        grid=(ROWS // TILE, D_IN // tn),
        in_specs=[
            pl.BlockSpec((TILE, D_MODEL), lambda i, j: (i, 0)),
            pl.BlockSpec((1, 1, N_MOD * D_MODEL), lambda i, j: (cond_of_tile(i), 0, 0)),
            pl.BlockSpec((1, D_MODEL), lambda i, j: (0, 0)),
            pl.BlockSpec((D_MODEL, tn), lambda i, j: (0, j)),
        ],
        out_specs=pl.BlockSpec((TILE, tn), lambda i, j: (i, j)),
        out_shape=jax.ShapeDtypeStruct((ROWS, D_IN), F32),
        scratch_shapes=[pltpu.VMEM((TILE, D_MODEL), BF16)],
        compiler_params=_params("arbitrary", "arbitrary"),
        name="inproj",
    )(x, mod_l, norm_g, w_in_bf)


PAD_B = 16
CONV_ROWS = 64


def _conv_kernel(p_ref, wa_ref, wb_ref, bb_ref, lng_ref, lnb_ref, out_ref, u_scr, c_scr, *, seq_len):
    latent = seq_len == GRID_W
    a_b = p_ref[:, 0:D_A]
    z = p_ref[:, D_A:2 * D_A] * p_ref[:, 2 * D_A:3 * D_A]
    pos = lax.broadcasted_iota(jnp.int32, (TILE, D_A), 0) % seq_len
    z_prev = jnp.where(pos == 0, 0.0, pltpu.roll(z, 1, axis=0))
    z_next = jnp.where(pos == seq_len - 1, 0.0, pltpu.roll(z, TILE - 1, axis=0))
    wa = wa_ref[...]
    y_a = a_b * (wa[0:1] * z_prev + wa[1:2] * z + wa[2:3] * z_next)
    out_ref[:, 0:D_A] = y_a.astype(out_ref.dtype)

    u = p_ref[:, 3 * D_A:3 * D_A + D_B] * _sigmoid_pair(p_ref[:, 3 * D_A + D_B:3 * D_A + 2 * D_B])[0]
    if latent:
        n_r = TILE // GRID_W
        u_scr[...] = u

        def row_body(r, carry):
            acc = jnp.zeros((GRID_W, D_B), F32)
            for rp in range(n_r):
                w = wb_ref[pl.ds(CONV_B // 2 + rp - r, 1), :]
                acc = acc + w * u_scr[rp * GRID_W:(rp + 1) * GRID_W, :]
            c_scr[pl.ds(pl.multiple_of(r * GRID_W, GRID_W), GRID_W), :] = acc
            return carry

        lax.fori_loop(0, n_r, row_body, 0)
    else:
        n_seq = TILE // seq_len
        padded = seq_len + 2 * PAD_B
        zeros = jnp.zeros((PAD_B, D_B), F32)
        for s in range(n_seq):
            u_scr[s * padded:s * padded + PAD_B, :] = zeros
            u_scr[s * padded + PAD_B:s * padded + PAD_B + seq_len, :] = u[s * seq_len:(s + 1) * seq_len]
            u_scr[s * padded + PAD_B + seq_len:(s + 1) * padded, :] = zeros
        for s in range(n_seq):
            for c in range(seq_len // CONV_ROWS):
                base = s * padded + c * CONV_ROWS + PAD_B - CONV_B // 2
                acc = jnp.zeros((CONV_ROWS, D_B), F32)
                for j in range(CONV_B):
                    acc = acc + wb_ref[j:j + 1, :] * u_scr[base + j:base + j + CONV_ROWS, :]
                c_scr[s * seq_len + c * CONV_ROWS:s * seq_len + (c + 1) * CONV_ROWS, :] = acc

    v = c_scr[...] + bb_ref[...]
    mu = jnp.mean(v, axis=-1, keepdims=True)
    d = v - mu
    var = jnp.mean(d * d, axis=-1, keepdims=True)
    ln = d * lax.rsqrt(var + EPS) * lng_ref[...] + lnb_ref[...]
    out_ref[:, D_A:D_A + D_B] = _silu(ln).astype(out_ref.dtype)


def _conv_mix(proj, conv_a_w, conv_b_w, conv_b_b, ln_g, ln_b, seq_len):
    n_seq = TILE // seq_len
    u_rows = TILE if seq_len == GRID_W else n_seq * (seq_len + 2 * PAD_B)
    wcols = 3 * D_A + 2 * D_B
    vec = lambda: pl.BlockSpec((1, D_B), lambda i: (0, 0))
    return pl.pallas_call(
        functools.partial(_conv_kernel, seq_len=seq_len),
        grid=(ROWS // TILE,),
        in_specs=[
            pl.BlockSpec((TILE, wcols), lambda i: (i, 0)),
            pl.BlockSpec((CONV_A, D_A), lambda i: (0, 0)),
            pl.BlockSpec((CONV_B, D_B), lambda i: (0, 0)),
            vec(), vec(), vec(),
        ],
        out_specs=pl.BlockSpec((TILE, D_A + D_B), lambda i: (i, 0)),
        out_shape=jax.ShapeDtypeStruct((ROWS, D_A + D_B), BF16),
        scratch_shapes=[pltpu.VMEM((u_rows, D_B), F32), pltpu.VMEM((TILE, D_B), F32)],
        compiler_params=_params("arbitrary"),
        name="conv_mix",
    )(proj, conv_a_w, conv_b_w, conv_b_b, ln_g, ln_b)


def _ref_rows(c, m, ref_pos, pos_in_block):
    size = 2 * m
    if size >= 8:
        pieces = [jnp.broadcast_to(c[b * size + ref_pos:b * size + ref_pos + 1, :], (size, HEAD))
                  for b in range(CHUNK // size)]
        return pieces[0] if len(pieces) == 1 else jnp.concatenate(pieces, axis=0)
    out = c
    for p in range(size):
        shift = p - ref_pos
        if shift == 0:
            continue
        out = jnp.where(pos_in_block == p, pltpu.roll(c, shift % CHUNK, axis=0), out)
    return out


def _hgrn_chunk(q, v_bf, x, lb, s_t, consts, backward):
    tri2, row, xor_rc = consts
    sg, sgn = _sigmoid_pair(x)
    one_m = 1.0 - lb
    lf = jnp.log(lb + one_m * sg)
    kk = one_m * sgn
    lf_hi = lf.astype(BF16)
    lf_lo = (lf - lf_hi.astype(F32)).astype(BF16)
    cb = _dot(tri2, jnp.concatenate([lf_hi, lf_lo], axis=0))
    total = cb[CHUNK - 1:CHUNK, :]
    c = cb - lf if backward else cb

    scores = jnp.where(xor_rc == 0, _dot_nt(q.astype(BF16), kk.astype(BF16)), 0.0)
    for b in range(N_LEVELS):
        m = 1 << b
        g = _ref_rows(c, m, m if backward else m - 1, row & (2 * m - 1))
        e = jnp.exp(-jnp.abs(c - g))
        upper = (row & m) != 0
        q_side = jnp.logical_not(upper) if backward else upper
        qt = jnp.where(q_side, q * e, 0.0).astype(BF16)
        kt = jnp.where(q_side, 0.0, kk * e).astype(BF16)
        part = _dot_nt(qt, kt)
        if 2 * m < CHUNK:
            part = jnp.where(xor_rc < 2 * m, part, 0.0)
        scores = scores + part

    if backward:
        q_dec = q * jnp.exp(total - c)
        k_dec = kk * jnp.exp(c)
    else:
        q_dec = q * jnp.exp(c)
        k_dec = kk * jnp.exp(total - c)
    o = _dot(scores.astype(BF16), v_bf) + _dot_nt(q_dec.astype(BF16), s_t.astype(BF16))
    s_new = s_t * jnp.exp(total) + _dot_tn(v_bf, k_dec.astype(BF16))
    return o, s_new


def _hgrn_kernel(*refs, n_seq, has_state_in, has_state_out):
    q_ref, i_ref, ff_ref, fb_ref, g_ref, lbf_ref, lbb_ref, gn_ref = refs[:8]
    k = 8
    s0_ref = None
    if has_state_in:
        s0_ref = refs[k]
        k += 1
    y_ref = refs[k]
    k += 1
    sout_ref = None
    if has_state_out:
        sout_ref = refs[k]
        k += 1
    q_scr, o_scr = refs[k], refs[k + 1]

    cps = TILE // CHUNK // n_seq
    q_scr[...] = _silu(q_ref[...]) * (HEAD ** -0.5)

    r_io = lax.broadcasted_iota(jnp.int32, (CHUNK, CHUNK), 0)
    c_io = lax.broadcasted_iota(jnp.int32, (CHUNK, CHUNK), 1)
    tri = jnp.where(c_io <= r_io, 1.0, 0.0).astype(BF16)
    consts = (jnp.concatenate([tri, tri], axis=1), r_io, r_io ^ c_io)

    for d, (f_ref, lb_ref) in enumerate(((ff_ref, lbf_ref), (fb_ref, lbb_ref))):
        backward = d == 1
        lb = lb_ref[0]

        for s in range(n_seq):
            if has_state_in:
                s_t0 = s0_ref[0, 0, d, 0].T
            else:
                s_t0 = jnp.zeros((HEAD, HEAD), F32)

            def chunk_body(j, s_t, s=s, backward=backward, f_ref=f_ref, lb=lb):
                ch = s * cps + ((cps - 1 - j) if backward else j)
                rows = pl.ds(pl.multiple_of(ch * CHUNK, CHUNK), CHUNK)
                o, s_new = _hgrn_chunk(q_scr[rows, :], i_ref[rows, :].astype(BF16), f_ref[rows, :],
                                       lb, s_t, consts, backward)
                if backward:
                    o_scr[rows, :] = o_scr[rows, :] + o
                else:
                    o_scr[rows, :] = o
                return s_new

            s_fin = lax.fori_loop(0, cps, chunk_body, s_t0)
            if has_state_out:
                sout_ref[s, d, 0] = s_fin.T

    o = o_scr[...]
    y = _rms(o) * gn_ref[0] * _silu(g_ref[...])
    y_ref[...] = y.astype(y_ref.dtype)


def _hgrn_mix(proj, lb_f, lb_b, gn_g, state_in, layer, n_seq):
    has_in = state_in is not None
    has_out = not has_in
    col0 = (3 * D_A + 2 * D_B) // HEAD

    def col(k):
        return pl.BlockSpec((TILE, HEAD), lambda i, h, k=k: (i, col0 + 4 * k + h))

    vec = lambda: pl.BlockSpec((1, 1, HEAD), lambda i, h: (h, 0, 0))
    in_specs = [col(0), col(1), col(2), col(3), col(4), vec(), vec(), vec()]
    args = [proj, proj, proj, proj, proj,
            lb_f.reshape(N_HEADS_C, 1, HEAD), lb_b.reshape(N_HEADS_C, 1, HEAD), gn_g.reshape(N_HEADS_C, 1, HEAD)]
    if has_in:
        in_specs.append(pl.BlockSpec((1, 1, 2, 1, HEAD, HEAD), lambda i, h: (i, layer, 0, h, 0, 0)))
        args.append(state_in)
    out_specs = [pl.BlockSpec((TILE, HEAD), lambda i, h: (i, h))]
    out_shape = [jax.ShapeDtypeStruct((ROWS, D_C), BF16)]
    if has_out:
        n_all = ROWS // TILE * n_seq
        out_specs.append(pl.BlockSpec((n_seq, 2, 1, HEAD, HEAD), lambda i, h: (i, 0, h, 0, 0)))
        out_shape.append(jax.ShapeDtypeStruct((n_all, 2, N_HEADS_C, HEAD, HEAD), F32))
    res = pl.pallas_call(
        functools.partial(_hgrn_kernel, n_seq=n_seq, has_state_in=has_in, has_state_out=has_out),
        grid=(ROWS // TILE, N_HEADS_C),
        in_specs=in_specs,
        out_specs=out_specs,
        out_shape=out_shape,
        scratch_shapes=[pltpu.VMEM((TILE, HEAD), F32), pltpu.VMEM((TILE, HEAD), F32)],
        compiler_params=_params("arbitrary", "arbitrary"),
        name="hgrn_mix",
    )(*args)
    return (res[0], res[1]) if has_out else (res[0], None)


TILE_E = 512


def _route(logits_t, bias_ref):
    n = logits_t.shape[1]
    lg = [logits_t[e:e + 1, :] for e in range(N_EXPERTS)]
    mx = functools.reduce(jnp.maximum, lg)
    ex = [jnp.exp(l - mx) for l in lg]
    den = functools.reduce(jnp.add, ex)
    sc = [e / den for e in ex]
    biased = [sc[e] + bias_ref[e:e + 1, :] for e in range(N_EXPERTS)]

    best = jnp.zeros((1, n), jnp.int32)
    best_val = None
    for g in range(N_GROUPS):
        mem = biased[g * EXPERTS_PER_GROUP:(g + 1) * EXPERTS_PER_GROUP]
        pair = None
        for a in range(EXPERTS_PER_GROUP):
            for b in range(a + 1, EXPERTS_PER_GROUP):
                sm = mem[a] + mem[b]
                pair = sm if pair is None else jnp.maximum(pair, sm)
        if best_val is None:
            best_val = pair
        else:
            upd = pair > best_val
            best = jnp.where(upd, g, best)
            best_val = jnp.where(upd, pair, best_val)

    masked = [jnp.where(best == (e // EXPERTS_PER_GROUP), biased[e], MASK_VALUE) for e in range(N_EXPERTS)]
    v1, i1 = masked[0], jnp.zeros((1, n), jnp.int32)
    for e in range(1, N_EXPERTS):
        upd = masked[e] > v1
        i1 = jnp.where(upd, e, i1)
        v1 = jnp.where(upd, masked[e], v1)
    v2 = jnp.full((1, n), -jnp.inf, F32)
    i2 = jnp.full((1, n), -1, jnp.int32)
    for e in range(N_EXPERTS):
        cand = jnp.where(i1 == e, -jnp.inf, masked[e])
        upd = cand > v2
        i2 = jnp.where(upd, e, i2)
        v2 = jnp.where(upd, cand, v2)
    w1 = functools.reduce(jnp.add, [jnp.where(i1 == e, sc[e], 0.0) for e in range(N_EXPERTS)])
    w2 = functools.reduce(jnp.add, [jnp.where(i2 == e, sc[e], 0.0) for e in range(N_EXPERTS)])
    inv = 1.0 / (w1 + w2)
    rows = [(jnp.where(i1 == e, w1, 0.0) + jnp.where(i2 == e, w2, 0.0)) * inv for e in range(N_EXPERTS)]
    return jnp.concatenate(rows, axis=0)


def _outproj_kernel(x_ref, yab_ref, yc_ref, mod_ref, g_ref, w_ref, wr_ref, br_ref, xo_ref, h_ref, gates_ref):
    m = mod_ref[0]
    g1 = m[:, 2 * D_MODEL:3 * D_MODEL]
    sh2 = m[:, 3 * D_MODEL:4 * D_MODEL]
    sc2 = m[:, 4 * D_MODEL:5 * D_MODEL]
    y = _dot(yab_ref[...], w_ref[0:D_A + D_B, :]) + _dot(yc_ref[...], w_ref[D_A + D_B:, :])
    x = x_ref[...] + g1 * y
    xo_ref[...] = x
    h = _rms(x) * g_ref[...] * (1.0 + sc2) + sh2
    h_ref[...] = h.astype(BF16)
    logits_t = lax.dot_general(wr_ref[...], h, (((1,), (1,)), ((), ())),
                               precision=lax.Precision.HIGHEST, preferred_element_type=F32)
    gates_ref[...] = _route(logits_t, br_ref)


def _outproj(x, y_ab, y_c, mod_l, norm_g, w_out_bf, w_router_t, b_router, cond_of_tile):
    per = TILE // TILE_E
    return pl.pallas_call(
        _outproj_kernel,
        grid=(ROWS // TILE_E,),
        in_specs=[
            pl.BlockSpec((TILE_E, D_MODEL), lambda i: (i, 0)),
            pl.BlockSpec((TILE_E, D_A + D_B), lambda i: (i, 0)),
            pl.BlockSpec((TILE_E, D_C), lambda i: (i, 0)),
            pl.BlockSpec((1, 1, N_MOD * D_MODEL), lambda i: (cond_of_tile(i // per), 0, 0)),
            pl.BlockSpec((1, D_MODEL), lambda i: (0, 0)),
            pl.BlockSpec((D_MODEL, D_MODEL), lambda i: (0, 0)),
            pl.BlockSpec((N_EXPERTS, D_MODEL), lambda i: (0, 0)),
            pl.BlockSpec((N_EXPERTS, 1), lambda i: (0, 0)),
        ],
        out_specs=[
            pl.BlockSpec((TILE_E, D_MODEL), lambda i: (i, 0)),
            pl.BlockSpec((TILE_E, D_MODEL), lambda i: (i, 0)),
            pl.BlockSpec((N_EXPERTS, TILE_E), lambda i: (0, i)),
        ],
        out_shape=[
            jax.ShapeDtypeStruct((ROWS, D_MODEL), F32),
            jax.ShapeDtypeStruct((ROWS, D_MODEL), BF16),
            jax.ShapeDtypeStruct((N_EXPERTS, ROWS), F32),
        ],
        compiler_params=_params("arbitrary"),
        name="outproj_router",
    )(x, y_ab, y_c, mod_l, norm_g, w_out_bf, w_router_t, b_router)


def _moe_kernel(x_ref, h_ref, gates_ref, mod_ref, wg_ref, wu_ref, wd_ref, fg_ref, out_ref, acc_scr, *, final):
    e = pl.program_id(1)

    @pl.when(e == 0)
    def _():
        acc_scr[...] = jnp.zeros_like(acc_scr)

    gates = gates_ref[...]
    lane = lax.broadcasted_iota(jnp.int32, gates.shape, 1)
    g = jnp.sum(jnp.where(lane == e, gates, 0.0), axis=1, keepdims=True)
    h = h_ref[...]
    hg = _dot(h, wg_ref[0])
    hu = _dot(h, wu_ref[0])
    act = (_silu(hg) * hu * g).astype(BF16)
    acc_scr[...] += _dot(act, wd_ref[0])

    @pl.when(e == N_EXPERTS - 1)
    def _():
        g2 = mod_ref[0][:, 5 * D_MODEL:6 * D_MODEL]
        x = x_ref[...] + g2 * acc_scr[...]
        if final:
            x = _rms(x) * fg_ref[...]
        out_ref[...] = x


def _moe(x, h, gates, mod_l, wg_bf, wu_bf, wd_bf, final_g, cond_of_tile, final):
    return pl.pallas_call(
        functools.partial(_moe_kernel, final=final),
        grid=(ROWS // TILE, N_EXPERTS),
        in_specs=[
            pl.BlockSpec((TILE, D_MODEL), lambda i, e: (i, 0)),
            pl.BlockSpec((TILE, D_MODEL), lambda i, e: (i, 0)),
            pl.BlockSpec((TILE, N_EXPERTS), lambda i, e: (i, 0)),
            pl.BlockSpec((1, 1, N_MOD * D_MODEL), lambda i, e: (cond_of_tile(i), 0, 0)),
            pl.BlockSpec((1, D_MODEL, D_EXPERT), lambda i, e: (e, 0, 0)),
            pl.BlockSpec((1, D_MODEL, D_EXPERT), lambda i, e: (e, 0, 0)),
            pl.BlockSpec((1, D_EXPERT, D_MODEL), lambda i, e: (e, 0, 0)),
            pl.BlockSpec((1, D_MODEL), lambda i, e: (0, 0)),
        ],
        out_specs=pl.BlockSpec((TILE, D_MODEL), lambda i, e: (i, 0)),
        out_shape=jax.ShapeDtypeStruct((ROWS, D_MODEL), F32),
        scratch_shapes=[pltpu.VMEM((TILE, D_MODEL), F32)],
        compiler_params=_params("arbitrary", "arbitrary"),
        name="moe",
    )(x, h, gates, mod_l, wg_bf, wu_bf, wd_bf, final_g)


def _run_stream(x, mod_all, cond_of_tile, state_in, seq_len, hgrn_seqs, lower_bound, p):
    states = []
    for l in range(DEPTH):
        mod_l = mod_all[l].reshape(8, 1, N_MOD * D_MODEL)
        proj = _inproj(x, mod_l, p["norm_mix_g"][l:l + 1], p["w_in"][l], cond_of_tile)
        y_ab = _conv_mix(proj, p["conv_a_w"][l], p["conv_b_w"][l], p["conv_b_b"][l:l + 1],
                         p["ln_b_g"][l:l + 1], p["ln_b_b"][l:l + 1], seq_len)
        y_c, s_new = _hgrn_mix(proj, lower_bound[0, l], lower_bound[1, l], p["gnorm_c_g"][l],
                               state_in, l, hgrn_seqs)
        if s_new is not None:
            states.append(s_new)
        x, h, gates_t = _outproj(x, y_ab, y_c, mod_l, p["norm_ffn_g"][l:l + 1], p["w_out"][l],
                                 p["w_router_t"], p["b_router"], cond_of_tile)
        x = _moe(x, h, gates_t.T, mod_l, p["w_gate"][l], p["w_up"][l], p["w_down"][l],
                 p["final_norm_g"], cond_of_tile, final=(l == DEPTH - 1))
    return x, states


def kernel(x_prompt, x_sample, state_hgrn, c, c_ctx, norm_mix_g, norm_ffn_g, final_norm_g, w_ada, b_ada, w_in, w_out, conv_a_w, conv_b_w, conv_b_b, ln_b_g, ln_b_b, lb_logits, gnorm_c_g, w_router, b_router, w_gate, w_up, w_down):
    batch, seq, _ = x_prompt.shape
    dec_batch, dec_seq, _ = x_sample.shape
    assert batch * seq == ROWS and dec_batch * dec_seq == ROWS and dec_seq == TILE and TILE % seq == 0

    p = dict(
        norm_mix_g=norm_mix_g, norm_ffn_g=norm_ffn_g, final_norm_g=final_norm_g.reshape(1, D_MODEL),
        w_in=w_in.astype(BF16), w_out=w_out.astype(BF16),
        conv_a_w=conv_a_w, conv_b_w=conv_b_w, conv_b_b=conv_b_b, ln_b_g=ln_b_g, ln_b_b=ln_b_b,
        gnorm_c_g=gnorm_c_g, w_router_t=w_router.T, b_router=b_router.reshape(N_EXPERTS, 1),
        w_gate=w_gate.astype(BF16), w_up=w_up.astype(BF16), w_down=w_down.astype(BF16),
    )
    lower_bound = _lower_bound(lb_logits)
    cond8 = jnp.concatenate([c_ctx[None, :], c, jnp.zeros((8 - 1 - dec_batch, D_MODEL), F32)], axis=0)
    mod_all = _adaln(cond8, w_ada, b_ada)

    y_p, states = _run_stream(x_prompt.reshape(ROWS, D_MODEL), mod_all, lambda i: 0, None,
                              seq, TILE // seq, lower_bound, p)
    y_s, _ = _run_stream(x_sample.reshape(ROWS, D_MODEL), mod_all, lambda i: i + 1, state_hgrn,
                         GRID_W, 1, lower_bound, p)
    new_state = jnp.stack(states, axis=1)
    return (y_p.reshape(batch, seq, D_MODEL), y_s.reshape(dec_batch, dec_seq, D_MODEL), new_state)
```

```python
import functools

import jax
import jax.numpy as jnp
from jax import lax
from jax.experimental import pallas as pl
from jax.experimental.pallas import tpu as pltpu

F32 = jnp.float32
BF16 = jnp.bfloat16

D_MODEL = 1024
DEPTH = 4
GRID_W = 64
D_A = 256
D_B = 256
D_C = 512
N_HEADS_C = 4
HEAD = 128
CONV_A = 3
CONV_B = 31
N_EXPERTS = 16
N_GROUPS = 4
EXPERTS_PER_GROUP = 4
D_EXPERT = 512
N_MOD = 6
D_IN = 3840
EPS = 1e-6
MASK_VALUE = -1e9
LOG2E = 1.4426950408889634

STREAM_ROWS = 4096
ROWS = 2 * STREAM_ROWS
TILE = 1024
STREAM_TILES = STREAM_ROWS // TILE
CHUNK = 128
N_LEVELS = 7
VMEM_LIMIT = 52 * 1024 * 1024


def _params(*sem):
    return pltpu.CompilerParams(dimension_semantics=sem, vmem_limit_bytes=VMEM_LIMIT)


def _dot(a, b):
    return jnp.dot(a, b, preferred_element_type=F32)


def _dot_nt(a, b):
    return lax.dot_general(a, b, (((1,), (1,)), ((), ())), preferred_element_type=F32)


def _dot_tn(a, b):
    return lax.dot_general(a, b, (((0,), (0,)), ((), ())), preferred_element_type=F32)


def _sigmoid_pair(x):
    e = jnp.exp(-jnp.abs(x))
    r = 1.0 / (1.0 + e)
    er = e * r
    pos = x >= 0
    return jnp.where(pos, r, er), jnp.where(pos, er, r)


def _silu(x):
    return x * _sigmoid_pair(x)[0]


def _rms(x):
    return x * lax.rsqrt(jnp.mean(x * x, axis=-1, keepdims=True) + EPS)


def _cond_of_tile(i):
    return jnp.maximum(i - (STREAM_TILES - 1), 0)


def _lower_bound_kernel(lg_ref, out_ref):
    for d in range(2):
        x = lg_ref[d]
        mx = jnp.max(x, axis=0, keepdims=True)
        e = jnp.exp(x - mx)
        p = e / jnp.sum(e, axis=0, keepdims=True)
        run = jnp.zeros_like(p[0:1])
        rows = []
        for l in range(DEPTH):
            run = run + p[l:l + 1]
            rows.append(run - p[0:1])
        out_ref[d] = jnp.concatenate(rows, axis=0)


def _lower_bound(lb_logits):
    return pl.pallas_call(
        _lower_bound_kernel,
        out_shape=jax.ShapeDtypeStruct((2, DEPTH, D_C), F32),
        name="lower_bound",
    )(lb_logits)


def _adaln_kernel(cond_ref, w_ref, b_ref, out_ref):
    a = _silu(cond_ref[...]).astype(BF16)
    out_ref[0] = _dot(a, w_ref[0].astype(BF16)) + b_ref[0]


def _adaln(cond8, w_ada, b_ada):
    tn = 1536
    nt = (N_MOD * D_MODEL) // tn
    return pl.pallas_call(
        _adaln_kernel,
        grid=(DEPTH, nt),
        in_specs=[
            pl.BlockSpec((8, D_MODEL), lambda l, j: (0, 0)),
            pl.BlockSpec((1, D_MODEL, tn), lambda l, j: (l, 0, j)),
            pl.BlockSpec((1, 1, tn), lambda l, j: (l, 0, j)),
        ],
        out_specs=pl.BlockSpec((1, 8, tn), lambda l, j: (l, 0, j)),
        out_shape=jax.ShapeDtypeStruct((DEPTH, 8, N_MOD * D_MODEL), F32),
        compiler_params=_params("arbitrary", "arbitrary"),
        name="adaln",
    )(cond8, w_ada, b_ada.reshape(DEPTH, 1, N_MOD * D_MODEL))


def _mod_spec(layer, tile_of):
    return pl.BlockSpec((1, 1, 1, N_MOD * D_MODEL),
                        lambda *g: (layer, _cond_of_tile(tile_of(*g)), 0, 0))


def _layer_vec_spec(layer, width):
    return pl.BlockSpec((1, 1, width), lambda *g: (layer, 0, 0))


def _inproj_kernel(x_ref, mod_ref, g_ref, w_ref, out_ref, h_scr):
    @pl.when(pl.program_id(1) == 0)
    def _():
        m = mod_ref[0, 0]
        sh = m[:, 0:D_MODEL]
        sc = m[:, D_MODEL:2 * D_MODEL]
        h = _rms(x_ref[...]) * g_ref[0] * (1.0 + sc) + sh
        h_scr[...] = h.astype(BF16)

    out_ref[...] = _dot(h_scr[...], w_ref[0])


def _inproj(x, mod, norm_g, w_in_bf, layer):
    tn = 768
    return pl.pallas_call(
        _inproj_kernel,
        grid=(ROWS // TILE, D_IN // tn),
        in_specs=[
            pl.BlockSpec((TILE, D_MODEL), lambda i, j: (i, 0)),
            _mod_spec(layer, lambda i, j: i),
            _layer_vec_spec(layer, D_MODEL),
            pl.BlockSpec((1, D_MODEL, tn), lambda i, j: (layer, 0, j)),
        ],
        out_specs=pl.BlockSpec((TILE, tn), lambda i, j: (i, j)),
        out_shape=jax.ShapeDtypeStruct((ROWS, D_IN), F32),
        scratch_shapes=[pltpu.VMEM((TILE, D_MODEL), BF16)],
        compiler_params=_params("arbitrary", "arbitrary"),
        name="inproj",
    )(x, mod, norm_g, w_in_bf)


PAD_B = 16
CONV_ROWS = 64


def _conv_kernel(p_ref, wa_ref, wb_ref, bb_ref, lng_ref, lnb_ref, *rest, seq_len):
    out_ref, u_scr, c_scr = rest[-3:]
    latent = seq_len == GRID_W
    a_b = p_ref[:, 0:D_A]
    z = p_ref[:, D_A:2 * D_A] * p_ref[:, 2 * D_A:3 * D_A]
    pos = lax.broadcasted_iota(jnp.int32, (TILE, D_A), 0) % seq_len
    z_prev = jnp.where(pos == 0, 0.0, pltpu.roll(z, 1, axis=0))
    z_next = jnp.where(pos == seq_len - 1, 0.0, pltpu.roll(z, TILE - 1, axis=0))
    wa = wa_ref[0]
    y_a = a_b * (wa[0:1] * z_prev + wa[1:2] * z + wa[2:3] * z_next)
    out_ref[:, 0:D_A] = y_a.astype(out_ref.dtype)

    u = p_ref[:, 3 * D_A:3 * D_A + D_B] * _sigmoid_pair(p_ref[:, 3 * D_A + D_B:3 * D_A + 2 * D_B])[0]
    if latent:
        n_r = TILE // GRID_W
        u_scr[...] = u

        def row_body(r, carry):
            acc = jnp.zeros((GRID_W, D_B), F32)
            for rp in range(n_r):
                w = wb_ref[0, pl.ds(CONV_B // 2 + rp - r, 1), :]
                acc = acc + w * u_scr[rp * GRID_W:(rp + 1) * GRID_W, :]
            c_scr[pl.ds(pl.multiple_of(r * GRID_W, GRID_W), GRID_W), :] = acc
            return carry

        lax.fori_loop(0, n_r, row_body, 0)
    else:
        n_seq = TILE // seq_len
        padded = seq_len + 2 * PAD_B
        zeros = jnp.zeros((PAD_B, D_B), F32)
        for s in range(n_seq):
            u_scr[s * padded:s * padded + PAD_B, :] = zeros
            u_scr[s * padded + PAD_B:s * padded + PAD_B + seq_len, :] = u[s * seq_len:(s + 1) * seq_len]
            u_scr[s * padded + PAD_B + seq_len:(s + 1) * padded, :] = zeros
        for s in range(n_seq):
            for c in range(seq_len // CONV_ROWS):
                base = s * padded + c * CONV_ROWS + PAD_B - CONV_B // 2
                acc = jnp.zeros((CONV_ROWS, D_B), F32)
                for j in range(CONV_B):
                    acc = acc + wb_ref[0, j:j + 1, :] * u_scr[base + j:base + j + CONV_ROWS, :]
                c_scr[s * seq_len + c * CONV_ROWS:s * seq_len + (c + 1) * CONV_ROWS, :] = acc

    v = c_scr[...] + bb_ref[0]
    mu = jnp.mean(v, axis=-1, keepdims=True)
    d = v - mu
    var = jnp.mean(d * d, axis=-1, keepdims=True)
    ln = d * lax.rsqrt(var + EPS) * lng_ref[0] + lnb_ref[0]
    out_ref[:, D_A:D_A + D_B] = _silu(ln).astype(out_ref.dtype)


def _conv_mix(proj, conv_a_w, conv_b_w, conv_b_b, ln_g, ln_b, layer, seq_len, tile0, prev_out):
    n_seq = TILE // seq_len
    u_rows = TILE if seq_len == GRID_W else n_seq * (seq_len + 2 * PAD_B)
    wcols = 3 * D_A + 2 * D_B
    in_specs = [
        pl.BlockSpec((TILE, wcols), lambda i: (i + tile0, 0)),
        pl.BlockSpec((1, CONV_A, D_A), lambda i: (layer, 0, 0)),
        pl.BlockSpec((1, CONV_B, D_B), lambda i: (layer, 0, 0)),
        _layer_vec_spec(layer, D_B), _layer_vec_spec(layer, D_B), _layer_vec_spec(layer, D_B),
    ]
    args = [proj, conv_a_w, conv_b_w, conv_b_b, ln_g, ln_b]
    aliases = {}
    if prev_out is not None:
        in_specs.append(pl.BlockSpec(memory_space=pl.ANY))
        args.append(prev_out)
        aliases = {len(args) - 1: 0}
    return pl.pallas_call(
        functools.partial(_conv_kernel, seq_len=seq_len),
        grid=(STREAM_TILES,),
        in_specs=in_specs,
        out_specs=pl.BlockSpec((TILE, D_A + D_B), lambda i: (i + tile0, 0)),
        out_shape=jax.ShapeDtypeStruct((ROWS, D_A + D_B), BF16),
        scratch_shapes=[pltpu.VMEM((u_rows, D_B), F32), pltpu.VMEM((TILE, D_B), F32)],
        input_output_aliases=aliases,
        compiler_params=_params("arbitrary"),
        name="conv_mix",
    )(*args)


HALF_C = 1280
CHAINS = 8


def _ref_rows(c, m, ref_pos, pos_in_block):
    size = 2 * m
    if size >= 8:
        pieces = [jnp.broadcast_to(c[b * size + ref_pos:b * size + ref_pos + 1, :], (size, HEAD))
                  for b in range(CHUNK // size)]
        return pieces[0] if len(pieces) == 1 else jnp.concatenate(pieces, axis=0)
    if size == 4:
        c3 = c.reshape(CHUNK // 8, 8, HEAD)
        sub = lax.broadcasted_iota(jnp.int32, c3.shape, 1)
        lo = jnp.broadcast_to(c3[:, ref_pos:ref_pos + 1, :], c3.shape)
        hi = jnp.broadcast_to(c3[:, 4 + ref_pos:5 + ref_pos, :], c3.shape)
        return jnp.where(sub < 4, lo, hi).reshape(CHUNK, HEAD)
    out = c
    for p in range(size):
        shift = p - ref_pos
        if shift == 0:
            continue
        out = jnp.where(pos_in_block == p, pltpu.roll(c, shift % CHUNK, axis=0), out)
    return out


def _hgrn_chunk(q, v_bf, x, lb, s_t, consts, backward):
    tri2, row, xor_rc = consts
    sg = 1.0 / (1.0 + jnp.exp2(x * (-LOG2E)))
    one_m = 1.0 - lb
    lf = jnp.log2(lb + one_m * sg)
    kk = one_m * (1.0 - sg)
    lf_hi = lf.astype(BF16)
    lf_lo = (lf - lf_hi.astype(F32)).astype(BF16)
    yield
    cb = _dot(tri2, jnp.concatenate([lf_hi, lf_lo], axis=0))
    total = cb[CHUNK - 1:CHUNK, :]
    c = cb - lf if backward else cb
    scores = _dot_nt(q.astype(BF16), kk.astype(BF16))
    yield

    for b in range(N_LEVELS):
        m = 1 << b
        if m < 8:
            g = _ref_rows(c, m, m if backward else m - 1, row & (2 * m - 1))
            e = jnp.exp2(-jnp.abs(c - g))
            qt, kt = (q * e).astype(BF16), (kk * e).astype(BF16)
        else:
            zeros = jnp.zeros((m, HEAD), F32)
            q_rows, k_rows = [], []
            for blk in range(CHUNK // (2 * m)):
                first = slice(2 * m * blk, 2 * m * blk + m)
                second = slice(2 * m * blk + m, 2 * m * (blk + 1))
                if backward:
                    g = c[2 * m * blk + m:2 * m * blk + m + 1, :]
                    q_rows += [q[first] * jnp.exp2(g - c[first]), zeros]
                    k_rows += [zeros, kk[second] * jnp.exp2(c[second] - g)]
                else:
                    g = c[2 * m * blk + m - 1:2 * m * blk + m, :]
                    q_rows += [zeros, q[second] * jnp.exp2(c[second] - g)]
                    k_rows += [kk[first] * jnp.exp2(g - c[first]), zeros]
            qt = jnp.concatenate(q_rows, axis=0).astype(BF16)
            kt = jnp.concatenate(k_rows, axis=0).astype(BF16)
        yield
        scores = jnp.where(xor_rc < m, scores, _dot_nt(qt, kt))
        yield
    col = row ^ xor_rc
    scores = jnp.where((row <= col) if backward else (row >= col), scores, 0.0).astype(BF16)

    if backward:
        q_dec = q * jnp.exp2(total - c)
        k_dec = kk * jnp.exp2(c)
    else:
        q_dec = q * jnp.exp2(c)
        k_dec = kk * jnp.exp2(total - c)
    q_dec, k_dec = q_dec.astype(BF16), k_dec.astype(BF16)
    yield
    o = _dot(scores, v_bf) + _dot_nt(q_dec, s_t.astype(BF16))
    s_new = s_t * jnp.exp2(total) + _dot_tn(v_bf, k_dec)
    return o, s_new


def _interleave(gens):
    results = [None] * len(gens)
    live = list(range(len(gens)))
    while live:
        still = []
        for i in live:
            try:
                next(gens[i])
                still.append(i)
            except StopIteration as stop:
                results[i] = stop.value
        live = still
    return results


def _hgrn_kernel(*refs, n_seq, has_state_in, has_prev_out):
    lo_ref, hi_ref, lb_ref, gn_ref = refs[:4]
    k = 4
    s0_ref = None
    if has_state_in:
        s0_ref = refs[k]
        k += 1
    k += int(has_prev_out)
    y_ref = refs[k]
    k += 1
    sout_ref = None
    if not has_state_in:
        sout_ref = refs[k]
        k += 1
    q_scr, of_scr, ob_scr, s_scr = refs[k:k + 4]

    def section(idx, h):
        off = idx * D_C + h * HEAD
        return (lo_ref, off) if off < HALF_C else (hi_ref, off - HALF_C)

    n_chunks = TILE // CHUNK
    cps = n_chunks // n_seq
    q_scr[...] = _silu(lo_ref[:, 0:D_C]) * (HEAD ** -0.5)

    r_io = lax.broadcasted_iota(jnp.int32, (CHUNK, CHUNK), 0)
    c_io = lax.broadcasted_iota(jnp.int32, (CHUNK, CHUNK), 1)
    tri = jnp.where(c_io <= r_io, 1.0, 0.0).astype(BF16)
    consts = (jnp.concatenate([tri, tri], axis=1), r_io, r_io ^ c_io)

    for h in range(N_HEADS_C):
        for d in range(2):
            if has_state_in:
                s_scr[2 * h + d] = s0_ref[0, 0, d, h].T
            else:
                s_scr[2 * h + d] = jnp.zeros((HEAD, HEAD), F32)

    def body(it, carry):
        s = it // cps
        j = it % cps
        chains = [(h, d) for h in range(N_HEADS_C) for d in range(2)]
        for g0 in range(0, len(chains), CHAINS):
            group = chains[g0:g0 + CHAINS]
            gens, rows_of = [], []
            for h, d in group:
                backward = d == 1
                ch = s * cps + ((cps - 1 - j) if backward else j)
                rows = pl.ds(pl.multiple_of(ch * CHUNK, CHUNK), CHUNK)
                i_ref, i_off = section(1, h)
                f_ref, f_off = section(2 + d, h)
                s_t = s_scr[2 * h + d]
                if not has_state_in:
                    s_t = jnp.where(j == 0, 0.0, s_t)
                gens.append(_hgrn_chunk(q_scr[rows, h * HEAD:(h + 1) * HEAD],
                                        i_ref[rows, i_off:i_off + HEAD].astype(BF16),
                                        f_ref[rows, f_off:f_off + HEAD],
                                        lb_ref[d, 0, :, h * HEAD:(h + 1) * HEAD], s_t, consts, backward))
                rows_of.append(rows)
            for (h, d), rows, (o, s_new) in zip(group, rows_of, _interleave(gens)):
                (ob_scr if d == 1 else of_scr)[rows, h * HEAD:(h + 1) * HEAD] = o
                s_scr[2 * h + d] = s_new
                if sout_ref is not None:
                    sout_ref[s, d, h] = s_new.T
        return carry

    lax.fori_loop(0, n_chunks, body, 0)

    for h in range(N_HEADS_C):
        cols = slice(h * HEAD, (h + 1) * HEAD)
        g_ref, g_off = section(4, h)
        o = of_scr[:, cols] + ob_scr[:, cols]
        y = _rms(o) * gn_ref[0, :, cols] * _silu(g_ref[:, g_off:g_off + HEAD])
        y_ref[:, cols] = y.astype(y_ref.dtype)


def _hgrn_mix(proj, lower_bound, gn_g, state_in, layer, n_seq, tile0, prev_out):
    has_in = state_in is not None
    blk0 = (3 * D_A + 2 * D_B) // HALF_C
    in_specs = [
        pl.BlockSpec((TILE, HALF_C), lambda i: (i + tile0, blk0)),
        pl.BlockSpec((TILE, HALF_C), lambda i: (i + tile0, blk0 + 1)),
        pl.BlockSpec((2, 1, 1, D_C), lambda i: (0, layer, 0, 0)),
        _layer_vec_spec(layer, D_C),
    ]
    args = [proj, proj, lower_bound, gn_g]
    if has_in:
        in_specs.append(pl.BlockSpec((1, 1, 2, N_HEADS_C, HEAD, HEAD), lambda i: (i, layer, 0, 0, 0, 0)))
        args.append(state_in)
    aliases = {}
    if prev_out is not None:
        in_specs.append(pl.BlockSpec(memory_space=pl.ANY))
        args.append(prev_out)
        aliases = {len(args) - 1: 0}
    out_specs = [pl.BlockSpec((TILE, D_C), lambda i: (i + tile0, 0))]
    out_shape = [jax.ShapeDtypeStruct((ROWS, D_C), BF16)]
    if not has_in:
        out_specs.append(pl.BlockSpec((n_seq, 2, N_HEADS_C, HEAD, HEAD), lambda i: (i, 0, 0, 0, 0)))
        out_shape.append(jax.ShapeDtypeStruct((STREAM_TILES * n_seq, 2, N_HEADS_C, HEAD, HEAD), F32))
    res = pl.pallas_call(
        functools.partial(_hgrn_kernel, n_seq=n_seq, has_state_in=has_in, has_prev_out=prev_out is not None),
        grid=(STREAM_TILES,),
        in_specs=in_specs,
        out_specs=out_specs,
        out_shape=out_shape,
        scratch_shapes=[pltpu.VMEM((TILE, D_C), F32), pltpu.VMEM((TILE, D_C), F32),
                        pltpu.VMEM((TILE, D_C), F32), pltpu.VMEM((2 * N_HEADS_C, HEAD, HEAD), F32)],
        input_output_aliases=aliases,
        compiler_params=_params("arbitrary"),
        name="hgrn_mix",
    )(*args)
    return (res[0], None) if has_in else (res[0], res[1])


TILE_E = 512


def _route(logits_t, bias_ref):
    n = logits_t.shape[1]
    lg = [logits_t[e:e + 1, :] for e in range(N_EXPERTS)]
    mx = functools.reduce(jnp.maximum, lg)
    ex = [jnp.exp(l - mx) for l in lg]
    den = functools.reduce(jnp.add, ex)
    sc = [e / den for e in ex]
    biased = [sc[e] + bias_ref[e:e + 1, :] for e in range(N_EXPERTS)]

    best = jnp.zeros((1, n), jnp.int32)
    best_val = None
    for g in range(N_GROUPS):
        mem = biased[g * EXPERTS_PER_GROUP:(g + 1) * EXPERTS_PER_GROUP]
        pair = None
        for a in range(EXPERTS_PER_GROUP):
            for b in range(a + 1, EXPERTS_PER_GROUP):
                sm = mem[a] + mem[b]
                pair = sm if pair is None else jnp.maximum(pair, sm)
        if best_val is None:
            best_val = pair
        else:
            upd = pair > best_val
            best = jnp.where(upd, g, best)
            best_val = jnp.where(upd, pair, best_val)

    masked = [jnp.where(best == (e // EXPERTS_PER_GROUP), biased[e], MASK_VALUE) for e in range(N_EXPERTS)]
    v1, i1 = masked[0], jnp.zeros((1, n), jnp.int32)
    for e in range(1, N_EXPERTS):
        upd = masked[e] > v1
        i1 = jnp.where(upd, e, i1)
        v1 = jnp.where(upd, masked[e], v1)
    v2 = jnp.full((1, n), -jnp.inf, F32)
    i2 = jnp.full((1, n), -1, jnp.int32)
    for e in range(N_EXPERTS):
        cand = jnp.where(i1 == e, -jnp.inf, masked[e])
        upd = cand > v2
        i2 = jnp.where(upd, e, i2)
        v2 = jnp.where(upd, cand, v2)
    w1 = functools.reduce(jnp.add, [jnp.where(i1 == e, sc[e], 0.0) for e in range(N_EXPERTS)])
    w2 = functools.reduce(jnp.add, [jnp.where(i2 == e, sc[e], 0.0) for e in range(N_EXPERTS)])
    inv = 1.0 / (w1 + w2)
    rows = [(jnp.where(i1 == e, w1, 0.0) + jnp.where(i2 == e, w2, 0.0)) * inv for e in range(N_EXPERTS)]
    return jnp.concatenate(rows, axis=0)


def _outproj_kernel(x_ref, yab_ref, yc_ref, mod_ref, g_ref, w_ref, wr_ref, br_ref, xo_ref, h_ref, gates_ref):
    m = mod_ref[0, 0]
    g1 = m[:, 2 * D_MODEL:3 * D_MODEL]
    sh2 = m[:, 3 * D_MODEL:4 * D_MODEL]
    sc2 = m[:, 4 * D_MODEL:5 * D_MODEL]
    y = _dot(yab_ref[...], w_ref[0, 0:D_A + D_B, :]) + _dot(yc_ref[...], w_ref[0, D_A + D_B:, :])
    x = x_ref[...] + g1 * y
    xo_ref[...] = x
    h = _rms(x) * g_ref[0] * (1.0 + sc2) + sh2
    h_ref[...] = h.astype(BF16)
    logits_t = lax.dot_general(wr_ref[...], h, (((1,), (1,)), ((), ())),
                               precision=lax.Precision.HIGHEST, preferred_element_type=F32)
    gates_ref[...] = _route(logits_t, br_ref)


def _outproj(x, y_ab, y_c, mod, norm_g, w_out_bf, w_router_t, b_router, layer):
    per = TILE // TILE_E
    return pl.pallas_call(
        _outproj_kernel,
        grid=(ROWS // TILE_E,),
        in_specs=[
            pl.BlockSpec((TILE_E, D_MODEL), lambda i: (i, 0)),
            pl.BlockSpec((TILE_E, D_A + D_B), lambda i: (i, 0)),
            pl.BlockSpec((TILE_E, D_C), lambda i: (i, 0)),
            _mod_spec(layer, lambda i: i // per),
            _layer_vec_spec(layer, D_MODEL),
            pl.BlockSpec((1, D_MODEL, D_MODEL), lambda i: (layer, 0, 0)),
            pl.BlockSpec((N_EXPERTS, D_MODEL), lambda i: (0, 0)),
            pl.BlockSpec((N_EXPERTS, 1), lambda i: (0, 0)),
        ],
        out_specs=[
            pl.BlockSpec((TILE_E, D_MODEL), lambda i: (i, 0)),
            pl.BlockSpec((TILE_E, D_MODEL), lambda i: (i, 0)),
            pl.BlockSpec((N_EXPERTS, TILE_E), lambda i: (0, i)),
        ],
        out_shape=[
            jax.ShapeDtypeStruct((ROWS, D_MODEL), F32),
            jax.ShapeDtypeStruct((ROWS, D_MODEL), BF16),
            jax.ShapeDtypeStruct((N_EXPERTS, ROWS), F32),
        ],
        compiler_params=_params("arbitrary"),
        name="outproj_router",
    )(x, y_ab, y_c, mod, norm_g, w_out_bf, w_router_t, b_router)


def _moe_kernel(x_ref, h_ref, gates_ref, mod_ref, wg_ref, wu_ref, wd_ref, fg_ref, out_ref, acc_scr, *, final):
    e = pl.program_id(1)

    @pl.when(e == 0)
    def _():
        acc_scr[...] = jnp.zeros_like(acc_scr)

    gates = gates_ref[...]
    lane = lax.broadcasted_iota(jnp.int32, gates.shape, 1)
    g = jnp.sum(jnp.where(lane == e, gates, 0.0), axis=1, keepdims=True)
    h = h_ref[...]
    hg = _dot(h, wg_ref[0, 0])
    hu = _dot(h, wu_ref[0, 0])
    act = (_silu(hg) * hu * g).astype(BF16)
    acc_scr[...] += _dot(act, wd_ref[0, 0])

    @pl.when(e == N_EXPERTS - 1)
    def _():
        g2 = mod_ref[0, 0][:, 5 * D_MODEL:6 * D_MODEL]
        x = x_ref[...] + g2 * acc_scr[...]
        if final:
            x = _rms(x) * fg_ref[...]
        out_ref[...] = x


def _moe(x, h, gates, mod, wg_bf, wu_bf, wd_bf, final_g, layer, final):
    return pl.pallas_call(
        functools.partial(_moe_kernel, final=final),
        grid=(ROWS // TILE, N_EXPERTS),
        in_specs=[
            pl.BlockSpec((TILE, D_MODEL), lambda i, e: (i, 0)),
            pl.BlockSpec((TILE, D_MODEL), lambda i, e: (i, 0)),
            pl.BlockSpec((TILE, N_EXPERTS), lambda i, e: (i, 0)),
            _mod_spec(layer, lambda i, e: i),
            pl.BlockSpec((1, 1, D_MODEL, D_EXPERT), lambda i, e: (layer, e, 0, 0)),
            pl.BlockSpec((1, 1, D_MODEL, D_EXPERT), lambda i, e: (layer, e, 0, 0)),
            pl.BlockSpec((1, 1, D_EXPERT, D_MODEL), lambda i, e: (layer, e, 0, 0)),
            pl.BlockSpec((1, D_MODEL), lambda i, e: (0, 0)),
        ],
        out_specs=pl.BlockSpec((TILE, D_MODEL), lambda i, e: (i, 0)),
        out_shape=jax.ShapeDtypeStruct((ROWS, D_MODEL), F32),
        scratch_shapes=[pltpu.VMEM((TILE, D_MODEL), F32)],
        compiler_params=_params("arbitrary", "arbitrary"),
        name="moe",
    )(x, h, gates, mod, wg_bf, wu_bf, wd_bf, final_g)


def kernel(x_prompt, x_sample, state_hgrn, c, c_ctx, norm_mix_g, norm_ffn_g, final_norm_g, w_ada, b_ada, w_in, w_out, conv_a_w, conv_b_w, conv_b_b, ln_b_g, ln_b_b, lb_logits, gnorm_c_g, w_router, b_router, w_gate, w_up, w_down):
    batch, seq, _ = x_prompt.shape
    dec_batch, dec_seq, _ = x_sample.shape
    assert batch * seq == STREAM_ROWS and dec_batch * dec_seq == STREAM_ROWS
    assert dec_seq == TILE and TILE % seq == 0 and dec_batch + 1 <= 8

    vec = lambda a: a.reshape(DEPTH, 1, a.shape[-1])
    w_in_bf, w_out_bf = w_in.astype(BF16), w_out.astype(BF16)
    wg_bf, wu_bf, wd_bf = w_gate.astype(BF16), w_up.astype(BF16), w_down.astype(BF16)
    w_router_t = w_router.T
    b_router2 = b_router.reshape(N_EXPERTS, 1)
    final_g = final_norm_g.reshape(1, D_MODEL)
    norm_mix, norm_ffn, gn_g = vec(norm_mix_g), vec(norm_ffn_g), vec(gnorm_c_g)
    cb_b, ln_g, ln_b = vec(conv_b_b), vec(ln_b_g), vec(ln_b_b)

    lower_bound = _lower_bound(lb_logits).reshape(2, DEPTH, 1, D_C)
    cond8 = jnp.concatenate([c_ctx[None, :], c, jnp.zeros((8 - 1 - dec_batch, D_MODEL), F32)], axis=0)
    mod = _adaln(cond8, w_ada, b_ada).reshape(DEPTH, 8, 1, N_MOD * D_MODEL)

    x = jnp.concatenate([x_prompt.reshape(STREAM_ROWS, D_MODEL), x_sample.reshape(STREAM_ROWS, D_MODEL)], axis=0)
    states = []
    for l in range(DEPTH):
        proj = _inproj(x, mod, norm_mix, w_in_bf, l)
        y_ab = _conv_mix(proj, conv_a_w, conv_b_w, cb_b, ln_g, ln_b, l, seq, 0, None)
        y_ab = _conv_mix(proj, conv_a_w, conv_b_w, cb_b, ln_g, ln_b, l, GRID_W, STREAM_TILES, y_ab)
        y_c, s_new = _hgrn_mix(proj, lower_bound, gn_g, None, l, TILE // seq, 0, None)
        y_c, _ = _hgrn_mix(proj, lower_bound, gn_g, state_hgrn, l, 1, STREAM_TILES, y_c)
        states.append(s_new)
        x, h, gates_t = _outproj(x, y_ab, y_c, mod, norm_ffn, w_out_bf, w_router_t, b_router2, l)
        x = _moe(x, h, gates_t.T, mod, wg_bf, wu_bf, wd_bf, final_g, l, final=(l == DEPTH - 1))
    new_state = jnp.stack(states, axis=1)
    return (x[:STREAM_ROWS].reshape(batch, seq, D_MODEL), x[STREAM_ROWS:].reshape(dec_batch, dec_seq, D_MODEL),
            new_state)
```

```python
import functools

import jax
import jax.numpy as jnp
from jax import lax
from jax.experimental import pallas as pl
from jax.experimental.pallas import tpu as pltpu

F32 = jnp.float32
BF16 = jnp.bfloat16

D_MODEL = 1024
DEPTH = 4
GRID_W = 64
D_A = 256
D_B = 256
D_C = 512
N_HEADS_C = 4
HEAD = 128
CONV_A = 3
CONV_B = 31
N_EXPERTS = 16
N_GROUPS = 4
EXPERTS_PER_GROUP = 4
D_EXPERT = 512
N_MOD = 6
D_IN = 3840
EPS = 1e-6
MASK_VALUE = -1e9
LOG2E = 1.4426950408889634

STREAM_ROWS = 4096
ROWS = 2 * STREAM_ROWS
TILE = 1024
STREAM_TILES = STREAM_ROWS // TILE
CHUNK = 128
N_LEVELS = 7
VMEM_LIMIT = 52 * 1024 * 1024


def _params(*sem):
    return pltpu.CompilerParams(dimension_semantics=sem, vmem_limit_bytes=VMEM_LIMIT)


def _dot(a, b):
    return jnp.dot(a, b, preferred_element_type=F32)


def _dot_nt(a, b):
    return lax.dot_general(a, b, (((1,), (1,)), ((), ())), preferred_element_type=F32)


def _dot_tn(a, b):
    return lax.dot_general(a, b, (((0,), (0,)), ((), ())), preferred_element_type=F32)


def _sigmoid_pair(x):
    e = jnp.exp(-jnp.abs(x))
    r = 1.0 / (1.0 + e)
    er = e * r
    pos = x >= 0
    return jnp.where(pos, r, er), jnp.where(pos, er, r)


def _silu(x):
    return x * _sigmoid_pair(x)[0]


def _rms(x):
    return x * lax.rsqrt(jnp.mean(x * x, axis=-1, keepdims=True) + EPS)


def _cond_of_tile(i):
    return jnp.maximum(i - (STREAM_TILES - 1), 0)


def _lower_bound_kernel(lg_ref, out_ref):
    for d in range(2):
        x = lg_ref[d]
        mx = jnp.max(x, axis=0, keepdims=True)
        e = jnp.exp(x - mx)
        p = e / jnp.sum(e, axis=0, keepdims=True)
        run = jnp.zeros_like(p[0:1])
        rows = []
        for l in range(DEPTH):
            run = run + p[l:l + 1]
            rows.append(run - p[0:1])
        out_ref[d] = jnp.concatenate(rows, axis=0)


def _lower_bound(lb_logits):
    return pl.pallas_call(
        _lower_bound_kernel,
        out_shape=jax.ShapeDtypeStruct((2, DEPTH, D_C), F32),
        name="lower_bound",
    )(lb_logits)


def _adaln_kernel(cond_ref, w_ref, b_ref, out_ref):
    a = _silu(cond_ref[...]).astype(BF16)
    out_ref[0] = _dot(a, w_ref[0].astype(BF16)) + b_ref[0]


def _adaln(cond8, w_ada, b_ada):
    tn = 1536
    nt = (N_MOD * D_MODEL) // tn
    return pl.pallas_call(
        _adaln_kernel,
        grid=(DEPTH, nt),
        in_specs=[
            pl.BlockSpec((8, D_MODEL), lambda l, j: (0, 0)),
            pl.BlockSpec((1, D_MODEL, tn), lambda l, j: (l, 0, j)),
            pl.BlockSpec((1, 1, tn), lambda l, j: (l, 0, j)),
        ],
        out_specs=pl.BlockSpec((1, 8, tn), lambda l, j: (l, 0, j)),
        out_shape=jax.ShapeDtypeStruct((DEPTH, 8, N_MOD * D_MODEL), F32),
        compiler_params=_params("arbitrary", "arbitrary"),
        name="adaln",
    )(cond8, w_ada, b_ada.reshape(DEPTH, 1, N_MOD * D_MODEL))


def _mod_spec(layer, tile_of):
    return pl.BlockSpec((1, 1, 1, N_MOD * D_MODEL),
                        lambda *g: (layer, _cond_of_tile(tile_of(*g)), 0, 0))


def _layer_vec_spec(layer, width):
    return pl.BlockSpec((1, 1, width), lambda *g: (layer, 0, 0))


def _inproj_kernel(x_ref, mod_ref, g_ref, w_ref, out_ref, h_scr):
    @pl.when(pl.program_id(1) == 0)
    def _():
        m = mod_ref[0, 0]
        sh = m[:, 0:D_MODEL]
        sc = m[:, D_MODEL:2 * D_MODEL]
        h = _rms(x_ref[...]) * g_ref[0] * (1.0 + sc) + sh
        h_scr[...] = h.astype(BF16)

    out_ref[...] = _dot(h_scr[...], w_ref[0])


def _inproj(x, mod, norm_g, w_in_bf, layer):
    tn = 768
    return pl.pallas_call(
        _inproj_kernel,
        grid=(ROWS // TILE, D_IN // tn),
        in_specs=[
            pl.BlockSpec((TILE, D_MODEL), lambda i, j: (i, 0)),
            _mod_spec(layer, lambda i, j: i),
            _layer_vec_spec(layer, D_MODEL),
            pl.BlockSpec((1, D_MODEL, tn), lambda i, j: (layer, 0, j)),
        ],
        out_specs=pl.BlockSpec((TILE, tn), lambda i, j: (i, j)),
        out_shape=jax.ShapeDtypeStruct((ROWS, D_IN), F32),
        scratch_shapes=[pltpu.VMEM((TILE, D_MODEL), BF16)],
        compiler_params=_params("arbitrary", "arbitrary"),
        name="inproj",
    )(x, mod, norm_g, w_in_bf)


PAD_B = 16
CONV_ROWS = 64


def _conv_kernel(p_ref, wa_ref, wb_ref, bb_ref, lng_ref, lnb_ref, *rest, seq_len):
    out_ref, u_scr, c_scr = rest[-3:]
    latent = seq_len == GRID_W
    a_b = p_ref[:, 0:D_A]
    z = p_ref[:, D_A:2 * D_A] * p_ref[:, 2 * D_A:3 * D_A]
    pos = lax.broadcasted_iota(jnp.int32, (TILE, D_A), 0) % seq_len
    z_prev = jnp.where(pos == 0, 0.0, pltpu.roll(z, 1, axis=0))
    z_next = jnp.where(pos == seq_len - 1, 0.0, pltpu.roll(z, TILE - 1, axis=0))
    wa = wa_ref[0]
    y_a = a_b * (wa[0:1] * z_prev + wa[1:2] * z + wa[2:3] * z_next)
    out_ref[:, 0:D_A] = y_a.astype(out_ref.dtype)

    u = p_ref[:, 3 * D_A:3 * D_A + D_B] * _sigmoid_pair(p_ref[:, 3 * D_A + D_B:3 * D_A + 2 * D_B])[0]
    if latent:
        n_r = TILE // GRID_W
        u_scr[...] = u

        def row_body(r, carry):
            acc = jnp.zeros((GRID_W, D_B), F32)
            for rp in range(n_r):
                w = wb_ref[0, pl.ds(CONV_B // 2 + rp - r, 1), :]
                acc = acc + w * u_scr[rp * GRID_W:(rp + 1) * GRID_W, :]
            c_scr[pl.ds(pl.multiple_of(r * GRID_W, GRID_W), GRID_W), :] = acc
            return carry

        lax.fori_loop(0, n_r, row_body, 0)
    else:
        n_seq = TILE // seq_len
        padded = seq_len + 2 * PAD_B
        zeros = jnp.zeros((PAD_B, D_B), F32)
        for s in range(n_seq):
            u_scr[s * padded:s * padded + PAD_B, :] = zeros
            u_scr[s * padded + PAD_B:s * padded + PAD_B + seq_len, :] = u[s * seq_len:(s + 1) * seq_len]
            u_scr[s * padded + PAD_B + seq_len:(s + 1) * padded, :] = zeros
        for s in range(n_seq):
            for c in range(seq_len // CONV_ROWS):
                base = s * padded + c * CONV_ROWS + PAD_B - CONV_B // 2
                acc = jnp.zeros((CONV_ROWS, D_B), F32)
                for j in range(CONV_B):
                    acc = acc + wb_ref[0, j:j + 1, :] * u_scr[base + j:base + j + CONV_ROWS, :]
                c_scr[s * seq_len + c * CONV_ROWS:s * seq_len + (c + 1) * CONV_ROWS, :] = acc

    v = c_scr[...] + bb_ref[0]
    mu = jnp.mean(v, axis=-1, keepdims=True)
    d = v - mu
    var = jnp.mean(d * d, axis=-1, keepdims=True)
    ln = d * lax.rsqrt(var + EPS) * lng_ref[0] + lnb_ref[0]
    out_ref[:, D_A:D_A + D_B] = _silu(ln).astype(out_ref.dtype)


def _conv_mix(proj, conv_a_w, conv_b_w, conv_b_b, ln_g, ln_b, layer, seq_len, tile0, prev_out):
    n_seq = TILE // seq_len
    u_rows = TILE if seq_len == GRID_W else n_seq * (seq_len + 2 * PAD_B)
    wcols = 3 * D_A + 2 * D_B
    in_specs = [
        pl.BlockSpec((TILE, wcols), lambda i: (i + tile0, 0)),
        pl.BlockSpec((1, CONV_A, D_A), lambda i: (layer, 0, 0)),
        pl.BlockSpec((1, CONV_B, D_B), lambda i: (layer, 0, 0)),
        _layer_vec_spec(layer, D_B), _layer_vec_spec(layer, D_B), _layer_vec_spec(layer, D_B),
    ]
    args = [proj, conv_a_w, conv_b_w, conv_b_b, ln_g, ln_b]
    aliases = {}
    if prev_out is not None:
        in_specs.append(pl.BlockSpec(memory_space=pl.ANY))
        args.append(prev_out)
        aliases = {len(args) - 1: 0}
    return pl.pallas_call(
        functools.partial(_conv_kernel, seq_len=seq_len),
        grid=(STREAM_TILES,),
        in_specs=in_specs,
        out_specs=pl.BlockSpec((TILE, D_A + D_B), lambda i: (i + tile0, 0)),
        out_shape=jax.ShapeDtypeStruct((ROWS, D_A + D_B), BF16),
        scratch_shapes=[pltpu.VMEM((u_rows, D_B), F32), pltpu.VMEM((TILE, D_B), F32)],
        input_output_aliases=aliases,
        compiler_params=_params("arbitrary"),
        name="conv_mix",
    )(*args)


HALF_C = 1280
CHAINS = 8


def _ref_rows(c, m, ref_pos, pos_in_block):
    size = 2 * m
    if size >= 8:
        pieces = [jnp.broadcast_to(c[b * size + ref_pos:b * size + ref_pos + 1, :], (size, HEAD))
                  for b in range(CHUNK // size)]
        return pieces[0] if len(pieces) == 1 else jnp.concatenate(pieces, axis=0)
    if size == 4:
        c3 = c.reshape(CHUNK // 8, 8, HEAD)
        sub = lax.broadcasted_iota(jnp.int32, c3.shape, 1)
        lo = jnp.broadcast_to(c3[:, ref_pos:ref_pos + 1, :], c3.shape)
        hi = jnp.broadcast_to(c3[:, 4 + ref_pos:5 + ref_pos, :], c3.shape)
        return jnp.where(sub < 4, lo, hi).reshape(CHUNK, HEAD)
    out = c
    for p in range(size):
        shift = p - ref_pos
        if shift == 0:
            continue
        out = jnp.where(pos_in_block == p, pltpu.roll(c, shift % CHUNK, axis=0), out)
    return out


def _hgrn_chunk(q, v_bf, x, lb, s_t, consts, backward):
    tri2, row, xor_rc = consts
    sg = 1.0 / (1.0 + jnp.exp2(x * (-LOG2E)))
    one_m = 1.0 - lb
    lf = jnp.log2(lb + one_m * sg)
    kk = one_m * (1.0 - sg)
    lf_hi = lf.astype(BF16)
    lf_lo = (lf - lf_hi.astype(F32)).astype(BF16)
    yield
    cb = _dot(tri2, jnp.concatenate([lf_hi, lf_lo], axis=0))
    total = cb[CHUNK - 1:CHUNK, :]
    c = cb - lf if backward else cb
    scores = _dot_nt(q.astype(BF16), kk.astype(BF16))
    yield

    for b in range(N_LEVELS):
        m = 1 << b
        if m < 8:
            g = _ref_rows(c, m, m if backward else m - 1, row & (2 * m - 1))
            e = jnp.exp2(-jnp.abs(c - g))
            qt, kt = (q * e).astype(BF16), (kk * e).astype(BF16)
        else:
            zeros = jnp.zeros((m, HEAD), F32)
            q_rows, k_rows = [], []
            for blk in range(CHUNK // (2 * m)):
                first = slice(2 * m * blk, 2 * m * blk + m)
                second = slice(2 * m * blk + m, 2 * m * (blk + 1))
                if backward:
                    g = c[2 * m * blk + m:2 * m * blk + m + 1, :]
                    q_rows += [q[first] * jnp.exp2(g - c[first]), zeros]
                    k_rows += [zeros, kk[second] * jnp.exp2(c[second] - g)]
                else:
                    g = c[2 * m * blk + m - 1:2 * m * blk + m, :]
                    q_rows += [zeros, q[second] * jnp.exp2(c[second] - g)]
                    k_rows += [kk[first] * jnp.exp2(g - c[first]), zeros]
            qt = jnp.concatenate(q_rows, axis=0).astype(BF16)
            kt = jnp.concatenate(k_rows, axis=0).astype(BF16)
        yield
        scores = jnp.where(xor_rc < m, scores, _dot_nt(qt, kt))
        yield
    col = row ^ xor_rc
    scores = jnp.where((row <= col) if backward else (row >= col), scores, 0.0).astype(BF16)

    if backward:
        q_dec = q * jnp.exp2(total - c)
        k_dec = kk * jnp.exp2(c)
    else:
        q_dec = q * jnp.exp2(c)
        k_dec = kk * jnp.exp2(total - c)
    q_dec, k_dec = q_dec.astype(BF16), k_dec.astype(BF16)
    yield
    o = _dot(scores, v_bf) + _dot_nt(q_dec, s_t.astype(BF16))
    s_new = s_t * jnp.exp2(total) + _dot_tn(v_bf, k_dec)
    return o, s_new


def _interleave(gens):
    results = [None] * len(gens)
    live = list(range(len(gens)))
    while live:
        still = []
        for i in live:
            try:
                next(gens[i])
                still.append(i)
            except StopIteration as stop:
                results[i] = stop.value
        live = still
    return results


def _hgrn_kernel(*refs, n_seq, has_state_in, has_prev_out):
    lo_ref, hi_ref, lb_ref, gn_ref = refs[:4]
    k = 4
    s0_ref = None
    if has_state_in:
        s0_ref = refs[k]
        k += 1
    k += int(has_prev_out)
    y_ref = refs[k]
    k += 1
    sout_ref = None
    if not has_state_in:
        sout_ref = refs[k]
        k += 1
    q_scr, of_scr, ob_scr, s_scr = refs[k:k + 4]

    def section(idx, h):
        off = idx * D_C + h * HEAD
        return (lo_ref, off) if off < HALF_C else (hi_ref, off - HALF_C)

    n_chunks = TILE // CHUNK
    cps = n_chunks // n_seq
    q_scr[...] = _silu(lo_ref[:, 0:D_C]) * (HEAD ** -0.5)

    r_io = lax.broadcasted_iota(jnp.int32, (CHUNK, CHUNK), 0)
    c_io = lax.broadcasted_iota(jnp.int32, (CHUNK, CHUNK), 1)
    tri = jnp.where(c_io <= r_io, 1.0, 0.0).astype(BF16)
    consts = (jnp.concatenate([tri, tri], axis=1), r_io, r_io ^ c_io)

    for h in range(N_HEADS_C):
        for d in range(2):
            if has_state_in:
                s_scr[2 * h + d] = s0_ref[0, 0, d, h].T
            else:
                s_scr[2 * h + d] = jnp.zeros((HEAD, HEAD), F32)

    def body(it, carry):
        s = it // cps
        j = it % cps
        chains = [(h, d) for h in range(N_HEADS_C) for d in range(2)]
        for g0 in range(0, len(chains), CHAINS):
            group = chains[g0:g0 + CHAINS]
            gens, rows_of = [], []
            for h, d in group:
                backward = d == 1
                ch = s * cps + ((cps - 1 - j) if backward else j)
                rows = pl.ds(pl.multiple_of(ch * CHUNK, CHUNK), CHUNK)
                i_ref, i_off = section(1, h)
                f_ref, f_off = section(2 + d, h)
                s_t = s_scr[2 * h + d]
                if not has_state_in:
                    s_t = jnp.where(j == 0, 0.0, s_t)
                gens.append(_hgrn_chunk(q_scr[rows, h * HEAD:(h + 1) * HEAD],
                                        i_ref[rows, i_off:i_off + HEAD].astype(BF16),
                                        f_ref[rows, f_off:f_off + HEAD],
                                        lb_ref[d, 0, :, h * HEAD:(h + 1) * HEAD], s_t, consts, backward))
                rows_of.append(rows)
            for (h, d), rows, (o, s_new) in zip(group, rows_of, _interleave(gens)):
                (ob_scr if d == 1 else of_scr)[rows, h * HEAD:(h + 1) * HEAD] = o
                s_scr[2 * h + d] = s_new
                if sout_ref is not None:
                    sout_ref[s, d, h] = s_new.T
        return carry

    lax.fori_loop(0, n_chunks, body, 0)

    for h in range(N_HEADS_C):
        cols = slice(h * HEAD, (h + 1) * HEAD)
        g_ref, g_off = section(4, h)
        o = of_scr[:, cols] + ob_scr[:, cols]
        y = _rms(o) * gn_ref[0, :, cols] * _silu(g_ref[:, g_off:g_off + HEAD])
        y_ref[:, cols] = y.astype(y_ref.dtype)


def _hgrn_mix(proj, lower_bound, gn_g, state_in, layer, n_seq, tile0, prev_out):
    has_in = state_in is not None
    blk0 = (3 * D_A + 2 * D_B) // HALF_C
    in_specs = [
        pl.BlockSpec((TILE, HALF_C), lambda i: (i + tile0, blk0)),
        pl.BlockSpec((TILE, HALF_C), lambda i: (i + tile0, blk0 + 1)),
        pl.BlockSpec((2, 1, 1, D_C), lambda i: (0, layer, 0, 0)),
        _layer_vec_spec(layer, D_C),
    ]
    args = [proj, proj, lower_bound, gn_g]
    if has_in:
        in_specs.append(pl.BlockSpec((1, 1, 2, N_HEADS_C, HEAD, HEAD), lambda i: (i, layer, 0, 0, 0, 0)))
        args.append(state_in)
    aliases = {}
    if prev_out is not None:
        in_specs.append(pl.BlockSpec(memory_space=pl.ANY))
        args.append(prev_out)
        aliases = {len(args) - 1: 0}
    out_specs = [pl.BlockSpec((TILE, D_C), lambda i: (i + tile0, 0))]
    out_shape = [jax.ShapeDtypeStruct((ROWS, D_C), BF16)]
    if not has_in:
        out_specs.append(pl.BlockSpec((n_seq, 2, N_HEADS_C, HEAD, HEAD), lambda i: (i, 0, 0, 0, 0)))
        out_shape.append(jax.ShapeDtypeStruct((STREAM_TILES * n_seq, 2, N_HEADS_C, HEAD, HEAD), F32))
    res = pl.pallas_call(
        functools.partial(_hgrn_kernel, n_seq=n_seq, has_state_in=has_in, has_prev_out=prev_out is not None),
        grid=(STREAM_TILES,),
        in_specs=in_specs,
        out_specs=out_specs,
        out_shape=out_shape,
        scratch_shapes=[pltpu.VMEM((TILE, D_C), F32), pltpu.VMEM((TILE, D_C), F32),
                        pltpu.VMEM((TILE, D_C), F32), pltpu.VMEM((2 * N_HEADS_C, HEAD, HEAD), F32)],
        input_output_aliases=aliases,
        compiler_params=_params("arbitrary"),
        name="hgrn_mix",
    )(*args)
    return (res[0], None) if has_in else (res[0], res[1])


TILE_E = 512


def _route(logits_t, bias_ref):
    n = logits_t.shape[1]
    lg = [logits_t[e:e + 1, :] for e in range(N_EXPERTS)]
    mx = functools.reduce(jnp.maximum, lg)
    ex = [jnp.exp(l - mx) for l in lg]
    den = functools.reduce(jnp.add, ex)
    sc = [e / den for e in ex]
    biased = [sc[e] + bias_ref[e:e + 1, :] for e in range(N_EXPERTS)]

    best = jnp.zeros((1, n), jnp.int32)
    best_val = None
    for g in range(N_GROUPS):
        mem = biased[g * EXPERTS_PER_GROUP:(g + 1) * EXPERTS_PER_GROUP]
        pair = None
        for a in range(EXPERTS_PER_GROUP):
            for b in range(a + 1, EXPERTS_PER_GROUP):
                sm = mem[a] + mem[b]
                pair = sm if pair is None else jnp.maximum(pair, sm)
        if best_val is None:
            best_val = pair
        else:
            upd = pair > best_val
            best = jnp.where(upd, g, best)
            best_val = jnp.where(upd, pair, best_val)

    masked = [jnp.where(best == (e // EXPERTS_PER_GROUP), biased[e], MASK_VALUE) for e in range(N_EXPERTS)]
    v1, i1 = masked[0], jnp.zeros((1, n), jnp.int32)
    for e in range(1, N_EXPERTS):
        upd = masked[e] > v1
        i1 = jnp.where(upd, e, i1)
        v1 = jnp.where(upd, masked[e], v1)
    v2 = jnp.full((1, n), -jnp.inf, F32)
    i2 = jnp.full((1, n), -1, jnp.int32)
    for e in range(N_EXPERTS):
        cand = jnp.where(i1 == e, -jnp.inf, masked[e])
        upd = cand > v2
        i2 = jnp.where(upd, e, i2)
        v2 = jnp.where(upd, cand, v2)
    w1 = functools.reduce(jnp.add, [jnp.where(i1 == e, sc[e], 0.0) for e in range(N_EXPERTS)])
    w2 = functools.reduce(jnp.add, [jnp.where(i2 == e, sc[e], 0.0) for e in range(N_EXPERTS)])
    inv = 1.0 / (w1 + w2)
    rows = [(jnp.where(i1 == e, w1, 0.0) + jnp.where(i2 == e, w2, 0.0)) * inv for e in range(N_EXPERTS)]
    return jnp.concatenate(rows, axis=0), best


def _outproj_kernel(x_ref, yab_ref, yc_ref, mod_ref, g_ref, w_ref, wr_ref, br_ref, xo_ref, h_ref, gates_ref,
                    gid_ref):
    m = mod_ref[0, 0]
    g1 = m[:, 2 * D_MODEL:3 * D_MODEL]
    sh2 = m[:, 3 * D_MODEL:4 * D_MODEL]
    sc2 = m[:, 4 * D_MODEL:5 * D_MODEL]
    y = _dot(yab_ref[...], w_ref[0, 0:D_A + D_B, :]) + _dot(yc_ref[...], w_ref[0, D_A + D_B:, :])
    x = x_ref[...] + g1 * y
    xo_ref[...] = x
    h = _rms(x) * g_ref[0] * (1.0 + sc2) + sh2
    h_ref[...] = h.astype(BF16)
    logits_t = lax.dot_general(wr_ref[...], h, (((1,), (1,)), ((), ())),
                               precision=lax.Precision.HIGHEST, preferred_element_type=F32)
    gates_ref[...], gid_ref[...] = _route(logits_t, br_ref)


def _outproj(x, y_ab, y_c, mod, norm_g, w_out_bf, w_router_t, b_router, layer):
    per = TILE // TILE_E
    return pl.pallas_call(
        _outproj_kernel,
        grid=(ROWS // TILE_E,),
        in_specs=[
            pl.BlockSpec((TILE_E, D_MODEL), lambda i: (i, 0)),
            pl.BlockSpec((TILE_E, D_A + D_B), lambda i: (i, 0)),
            pl.BlockSpec((TILE_E, D_C), lambda i: (i, 0)),
            _mod_spec(layer, lambda i: i // per),
            _layer_vec_spec(layer, D_MODEL),
            pl.BlockSpec((1, D_MODEL, D_MODEL), lambda i: (layer, 0, 0)),
            pl.BlockSpec((N_EXPERTS, D_MODEL), lambda i: (0, 0)),
            pl.BlockSpec((N_EXPERTS, 1), lambda i: (0, 0)),
        ],
        out_specs=[
            pl.BlockSpec((TILE_E, D_MODEL), lambda i: (i, 0)),
            pl.BlockSpec((TILE_E, D_MODEL), lambda i: (i, 0)),
            pl.BlockSpec((N_EXPERTS, TILE_E), lambda i: (0, i)),
            pl.BlockSpec((1, TILE_E), lambda i: (0, i)),
        ],
        out_shape=[
            jax.ShapeDtypeStruct((ROWS, D_MODEL), F32),
            jax.ShapeDtypeStruct((ROWS, D_MODEL), BF16),
            jax.ShapeDtypeStruct((N_EXPERTS, ROWS), F32),
            jax.ShapeDtypeStruct((1, ROWS), jnp.int32),
        ],
        compiler_params=_params("arbitrary"),
        name="outproj_router",
    )(x, y_ab, y_c, mod, norm_g, w_out_bf, w_router_t, b_router)


SUB = 128
MOE_VMEM_LIMIT = 58 * 1024 * 1024


def _plan_kernel(gid_ref, pos_ref, tab_ref):
    gid = gid_ref[...]
    grp = lax.broadcasted_iota(jnp.int32, (8, TILE), 0)
    onehot = jnp.where(grp == gid, 1.0, 0.0)
    r_io = lax.broadcasted_iota(jnp.int32, (TILE, TILE), 0)
    c_io = lax.broadcasted_iota(jnp.int32, (TILE, TILE), 1)
    before = jnp.where(r_io < c_io, 1.0, 0.0).astype(BF16)
    rank = _dot(onehot.astype(BF16), before)
    n = jnp.sum(onehot, axis=1, keepdims=True)
    cnt = jnp.floor((n + (SUB - 1.0)) * (1.0 / SUB))
    run = jnp.zeros((1, 1), F32)
    starts = []
    for g in range(N_GROUPS):
        starts.append(run)
        run = run + cnt[g:g + 1]
    start = jnp.concatenate(starts + [jnp.zeros((8 - N_GROUPS, 1), F32)], axis=0)
    pos = jnp.sum(onehot * (start * SUB + rank), axis=0, keepdims=True)
    pos_ref[...] = pos.astype(jnp.int32)
    tab = jnp.concatenate([start[0:N_GROUPS], cnt[0:N_GROUPS]], axis=0)
    tab_ref[0] = jnp.broadcast_to(tab, (8, 128)).astype(jnp.int32)


def _plan(gid):
    n_blk = ROWS // TILE
    return pl.pallas_call(
        _plan_kernel,
        grid=(n_blk,),
        in_specs=[pl.BlockSpec((1, TILE), lambda i: (0, i))],
        out_specs=[pl.BlockSpec((1, TILE), lambda i: (0, i)), pl.BlockSpec((1, 8, 128), lambda i: (i, 0, 0))],
        out_shape=[jax.ShapeDtypeStruct((1, ROWS), jnp.int32), jax.ShapeDtypeStruct((n_blk, 8, 128), jnp.int32)],
        compiler_params=_params("arbitrary"),
        name="moe_plan",
    )(gid)


def _moe_kernel(tab_ref, x_ref, h_ref, pos_ref, gates_ref, mod_ref, wg_ref, wu_ref, wd_ref, fg_ref, out_ref, *, final):
    blk = pl.program_id(0)
    grp = pl.program_id(1)

    @pl.when(grp == 0)
    def _():
        out_ref[...] = jnp.zeros_like(out_ref)

    first = tab_ref[blk, grp]
    count = tab_ref[blk, N_GROUPS + grp]
    gates = gates_ref[0]
    g_hi = gates.astype(BF16)
    g_lo = (gates - g_hi.astype(F32)).astype(BF16)
    g_split = jnp.concatenate([g_hi, g_lo], axis=0)
    r_io = lax.broadcasted_iota(jnp.int32, (SUB, TILE), 0)

    def sub_tile(s, carry):
        sel = jnp.where(r_io == pos_ref[...] - (first + s) * SUB, 1.0, 0.0).astype(BF16)
        xs = _dot(sel, h_ref[...]).astype(BF16)
        gs = _dot_nt(sel, g_split)
        y = jnp.zeros((SUB, D_MODEL), F32)
        for e in range(EXPERTS_PER_GROUP):
            gate = gs[:, e:e + 1] + gs[:, EXPERTS_PER_GROUP + e:EXPERTS_PER_GROUP + e + 1]
            act = _silu(_dot(xs, wg_ref[0, e])) * _dot(xs, wu_ref[0, e]) * gate
            y = y + _dot(act.astype(BF16), wd_ref[0, e])
        y_hi = y.astype(BF16)
        y_lo = (y - y_hi.astype(F32)).astype(BF16)
        out_ref[...] += _dot_tn(jnp.concatenate([sel, sel], axis=0), jnp.concatenate([y_hi, y_lo], axis=0))
        return carry

    lax.fori_loop(0, count, sub_tile, 0)

    @pl.when(grp == N_GROUPS - 1)
    def _():
        g2 = mod_ref[0, 0][:, 5 * D_MODEL:6 * D_MODEL]
        x = x_ref[...] + g2 * out_ref[...]
        if final:
            x = _rms(x) * fg_ref[...]
        out_ref[...] = x


def _moe(x, h, pos, tab, gates, mod, wg_bf, wu_bf, wd_bf, final_g, layer, final):
    grid_spec = pltpu.PrefetchScalarGridSpec(
        num_scalar_prefetch=1,
        grid=(ROWS // TILE, N_GROUPS),
        in_specs=[
            pl.BlockSpec((TILE, D_MODEL), lambda i, g, t: (i, 0)),
            pl.BlockSpec((TILE, D_MODEL), lambda i, g, t: (i, 0)),
            pl.BlockSpec((1, TILE), lambda i, g, t: (0, i)),
            pl.BlockSpec((1, EXPERTS_PER_GROUP, TILE), lambda i, g, t: (g, 0, i)),
            pl.BlockSpec((1, 1, 1, N_MOD * D_MODEL), lambda i, g, t: (layer, _cond_of_tile(i), 0, 0)),
            pl.BlockSpec((1, EXPERTS_PER_GROUP, D_MODEL, D_EXPERT), lambda i, g, t: (layer, g, 0, 0)),
            pl.BlockSpec((1, EXPERTS_PER_GROUP, D_MODEL, D_EXPERT), lambda i, g, t: (layer, g, 0, 0)),
            pl.BlockSpec((1, EXPERTS_PER_GROUP, D_EXPERT, D_MODEL), lambda i, g, t: (layer, g, 0, 0)),
            pl.BlockSpec((1, D_MODEL), lambda i, g, t: (0, 0)),
        ],
        out_specs=pl.BlockSpec((TILE, D_MODEL), lambda i, g, t: (i, 0)),
    )
    return pl.pallas_call(
        functools.partial(_moe_kernel, final=final),
        grid_spec=grid_spec,
        out_shape=jax.ShapeDtypeStruct((ROWS, D_MODEL), F32),
        compiler_params=pltpu.CompilerParams(dimension_semantics=("arbitrary", "arbitrary"),
                                             vmem_limit_bytes=MOE_VMEM_LIMIT),
        name="moe",
    )(tab, x, h, pos, gates, mod, wg_bf, wu_bf, wd_bf, final_g)


def kernel(x_prompt, x_sample, state_hgrn, c, c_ctx, norm_mix_g, norm_ffn_g, final_norm_g, w_ada, b_ada, w_in, w_out, conv_a_w, conv_b_w, conv_b_b, ln_b_g, ln_b_b, lb_logits, gnorm_c_g, w_router, b_router, w_gate, w_up, w_down):
    batch, seq, _ = x_prompt.shape
    dec_batch, dec_seq, _ = x_sample.shape
    assert batch * seq == STREAM_ROWS and dec_batch * dec_seq == STREAM_ROWS
    assert dec_seq == TILE and TILE % seq == 0 and dec_batch + 1 <= 8

    vec = lambda a: a.reshape(DEPTH, 1, a.shape[-1])
    w_in_bf, w_out_bf = w_in.astype(BF16), w_out.astype(BF16)
    wg_bf, wu_bf, wd_bf = w_gate.astype(BF16), w_up.astype(BF16), w_down.astype(BF16)
    w_router_t = w_router.T
    b_router2 = b_router.reshape(N_EXPERTS, 1)
    final_g = final_norm_g.reshape(1, D_MODEL)
    norm_mix, norm_ffn, gn_g = vec(norm_mix_g), vec(norm_ffn_g), vec(gnorm_c_g)
    cb_b, ln_g, ln_b = vec(conv_b_b), vec(ln_b_g), vec(ln_b_b)

    lower_bound = _lower_bound(lb_logits).reshape(2, DEPTH, 1, D_C)
    cond8 = jnp.concatenate([c_ctx[None, :], c, jnp.zeros((8 - 1 - dec_batch, D_MODEL), F32)], axis=0)
    mod = _adaln(cond8, w_ada, b_ada).reshape(DEPTH, 8, 1, N_MOD * D_MODEL)

    x = jnp.concatenate([x_prompt.reshape(STREAM_ROWS, D_MODEL), x_sample.reshape(STREAM_ROWS, D_MODEL)], axis=0)
    states = []
    for l in range(DEPTH):
        proj = _inproj(x, mod, norm_mix, w_in_bf, l)
        y_ab = _conv_mix(proj, conv_a_w, conv_b_w, cb_b, ln_g, ln_b, l, seq, 0, None)
        y_ab = _conv_mix(proj, conv_a_w, conv_b_w, cb_b, ln_g, ln_b, l, GRID_W, STREAM_TILES, y_ab)
        y_c, s_new = _hgrn_mix(proj, lower_bound, gn_g, None, l, TILE // seq, 0, None)
        y_c, _ = _hgrn_mix(proj, lower_bound, gn_g, state_hgrn, l, 1, STREAM_TILES, y_c)
        states.append(s_new)
        x, h, gates_t, gid = _outproj(x, y_ab, y_c, mod, norm_ffn, w_out_bf, w_router_t, b_router2, l)
        pos, tab = _plan(gid)
        x = _moe(x, h, pos, tab[:, :, 0], gates_t.reshape(N_GROUPS, EXPERTS_PER_GROUP, ROWS), mod,
                 wg_bf, wu_bf, wd_bf, final_g, l, final=(l == DEPTH - 1))
    new_state = jnp.stack(states, axis=1)
    return (x[:STREAM_ROWS].reshape(batch, seq, D_MODEL), x[STREAM_ROWS:].reshape(dec_batch, dec_seq, D_MODEL),
            new_state)
```

```python
import functools

import jax
import jax.numpy as jnp
from jax import lax
from jax.experimental import pallas as pl
from jax.experimental.pallas import tpu as pltpu

F32 = jnp.float32
BF16 = jnp.bfloat16

D_MODEL = 1024
DEPTH = 4
GRID_W = 64
D_A = 256
D_B = 256
D_C = 512
N_HEADS_C = 4
HEAD = 128
CONV_A = 3
CONV_B = 31
N_EXPERTS = 16
N_GROUPS = 4
EXPERTS_PER_GROUP = 4
D_EXPERT = 512
N_MOD = 6
D_IN = 3840
EPS = 1e-6
MASK_VALUE = -1e9
LOG2E = 1.4426950408889634

STREAM_ROWS = 4096
ROWS = 2 * STREAM_ROWS
TILE = 1024
STREAM_TILES = STREAM_ROWS // TILE
CHUNK = 128
N_LEVELS = 7
VMEM_LIMIT = 52 * 1024 * 1024


def _params(*sem):
    return pltpu.CompilerParams(dimension_semantics=sem, vmem_limit_bytes=VMEM_LIMIT)


def _dot(a, b):
    return jnp.dot(a, b, preferred_element_type=F32)


def _dot_nt(a, b):
    return lax.dot_general(a, b, (((1,), (1,)), ((), ())), preferred_element_type=F32)


def _dot_tn(a, b):
    return lax.dot_general(a, b, (((0,), (0,)), ((), ())), preferred_element_type=F32)


def _sigmoid(x):
    return 1.0 / (1.0 + jnp.exp2(x * (-LOG2E)))


def _silu(x):
    return x * _sigmoid(x)


def _rms(x):
    return x * lax.rsqrt(jnp.mean(x * x, axis=-1, keepdims=True) + EPS)


def _cond_of_tile(i):
    return jnp.maximum(i - (STREAM_TILES - 1), 0)


def _lower_bound_kernel(lg_ref, out_ref):
    for d in range(2):
        x = lg_ref[d]
        mx = jnp.max(x, axis=0, keepdims=True)
        e = jnp.exp(x - mx)
        p = e / jnp.sum(e, axis=0, keepdims=True)
        run = jnp.zeros_like(p[0:1])
        rows = []
        for l in range(DEPTH):
            run = run + p[l:l + 1]
            rows.append(run - p[0:1])
        out_ref[d] = jnp.concatenate(rows, axis=0)


def _lower_bound(lb_logits):
    return pl.pallas_call(
        _lower_bound_kernel,
        out_shape=jax.ShapeDtypeStruct((2, DEPTH, D_C), F32),
        name="lower_bound",
    )(lb_logits)


def _adaln_kernel(cond_ref, w_ref, b_ref, out_ref):
    a = _silu(cond_ref[...]).astype(BF16)
    out_ref[0] = _dot(a, w_ref[0].astype(BF16)) + b_ref[0]


def _adaln(cond8, w_ada, b_ada):
    tn = 1536
    nt = (N_MOD * D_MODEL) // tn
    return pl.pallas_call(
        _adaln_kernel,
        grid=(DEPTH, nt),
        in_specs=[
            pl.BlockSpec((8, D_MODEL), lambda l, j: (0, 0)),
            pl.BlockSpec((1, D_MODEL, tn), lambda l, j: (l, 0, j)),
            pl.BlockSpec((1, 1, tn), lambda l, j: (l, 0, j)),
        ],
        out_specs=pl.BlockSpec((1, 8, tn), lambda l, j: (l, 0, j)),
        out_shape=jax.ShapeDtypeStruct((DEPTH, 8, N_MOD * D_MODEL), F32),
        compiler_params=_params("arbitrary", "arbitrary"),
        name="adaln",
    )(cond8, w_ada, b_ada.reshape(DEPTH, 1, N_MOD * D_MODEL))


def _mod_spec(layer, tile_of):
    return pl.BlockSpec((1, 1, 1, N_MOD * D_MODEL),
                        lambda *g: (layer, _cond_of_tile(tile_of(*g)), 0, 0))


def _layer_vec_spec(layer, width):
    return pl.BlockSpec((1, 1, width), lambda *g: (layer, 0, 0))


def _inproj_kernel(x_ref, mod_ref, g_ref, w_ref, out_ref, h_scr):
    @pl.when(pl.program_id(1) == 0)
    def _():
        m = mod_ref[0, 0]
        sh = m[:, 0:D_MODEL]
        sc = m[:, D_MODEL:2 * D_MODEL]
        h = _rms(x_ref[...]) * g_ref[0] * (1.0 + sc) + sh
        h_scr[...] = h.astype(BF16)

    out_ref[...] = _dot(h_scr[...], w_ref[0])


def _inproj(x, mod, norm_g, w_in_bf, layer):
    tn = 768
    return pl.pallas_call(
        _inproj_kernel,
        grid=(ROWS // TILE, D_IN // tn),
        in_specs=[
            pl.BlockSpec((TILE, D_MODEL), lambda i, j: (i, 0)),
            _mod_spec(layer, lambda i, j: i),
            _layer_vec_spec(layer, D_MODEL),
            pl.BlockSpec((1, D_MODEL, tn), lambda i, j: (layer, 0, j)),
        ],
        out_specs=pl.BlockSpec((TILE, tn), lambda i, j: (i, j)),
        out_shape=jax.ShapeDtypeStruct((ROWS, D_IN), F32),
        scratch_shapes=[pltpu.VMEM((TILE, D_MODEL), BF16)],
        compiler_params=_params("arbitrary", "arbitrary"),
        name="inproj",
    )(x, mod, norm_g, w_in_bf)


PAD_B = 16
CONV_ROWS = 64


def _conv_kernel(p_ref, wa_ref, wb_ref, bb_ref, lng_ref, lnb_ref, *rest, seq_len):
    out_ref, u_scr, c_scr = rest[-3:]
    latent = seq_len == GRID_W
    a_b = p_ref[:, 0:D_A]
    z = p_ref[:, D_A:2 * D_A] * p_ref[:, 2 * D_A:3 * D_A]
    pos = lax.broadcasted_iota(jnp.int32, (TILE, D_A), 0) % seq_len
    z_prev = jnp.where(pos == 0, 0.0, pltpu.roll(z, 1, axis=0))
    z_next = jnp.where(pos == seq_len - 1, 0.0, pltpu.roll(z, TILE - 1, axis=0))
    wa = wa_ref[0]
    y_a = a_b * (wa[0:1] * z_prev + wa[1:2] * z + wa[2:3] * z_next)
    out_ref[:, 0:D_A] = y_a.astype(out_ref.dtype)

    u = p_ref[:, 3 * D_A:3 * D_A + D_B] * _sigmoid(p_ref[:, 3 * D_A + D_B:3 * D_A + 2 * D_B])
    if latent:
        n_r = TILE // GRID_W
        u_scr[...] = u

        def row_body(r, carry):
            acc = jnp.zeros((GRID_W, D_B), F32)
            for rp in range(n_r):
                w = wb_ref[0, pl.ds(CONV_B // 2 + rp - r, 1), :]
                acc = acc + w * u_scr[rp * GRID_W:(rp + 1) * GRID_W, :]
            c_scr[pl.ds(pl.multiple_of(r * GRID_W, GRID_W), GRID_W), :] = acc
            return carry

        lax.fori_loop(0, n_r, row_body, 0)
    else:
        n_seq = TILE // seq_len
        padded = seq_len + 2 * PAD_B
        zeros = jnp.zeros((PAD_B, D_B), F32)
        for s in range(n_seq):
            u_scr[s * padded:s * padded + PAD_B, :] = zeros
            u_scr[s * padded + PAD_B:s * padded + PAD_B + seq_len, :] = u[s * seq_len:(s + 1) * seq_len]
            u_scr[s * padded + PAD_B + seq_len:(s + 1) * padded, :] = zeros
        for s in range(n_seq):
            for c in range(seq_len // CONV_ROWS):
                base = s * padded + c * CONV_ROWS + PAD_B - CONV_B // 2
                acc = jnp.zeros((CONV_ROWS, D_B), F32)
                for j in range(CONV_B):
                    acc = acc + wb_ref[0, j:j + 1, :] * u_scr[base + j:base + j + CONV_ROWS, :]
                c_scr[s * seq_len + c * CONV_ROWS:s * seq_len + (c + 1) * CONV_ROWS, :] = acc

    v = c_scr[...] + bb_ref[0]
    mu = jnp.mean(v, axis=-1, keepdims=True)
    d = v - mu
    var = jnp.mean(d * d, axis=-1, keepdims=True)
    ln = d * lax.rsqrt(var + EPS) * lng_ref[0] + lnb_ref[0]
    out_ref[:, D_A:D_A + D_B] = _silu(ln).astype(out_ref.dtype)


def _conv_mix(proj, conv_a_w, conv_b_w, conv_b_b, ln_g, ln_b, layer, seq_len, tile0, prev_out):
    n_seq = TILE // seq_len
    u_rows = TILE if seq_len == GRID_W else n_seq * (seq_len + 2 * PAD_B)
    wcols = 3 * D_A + 2 * D_B
    in_specs = [
        pl.BlockSpec((TILE, wcols), lambda i: (i + tile0, 0)),
        pl.BlockSpec((1, CONV_A, D_A), lambda i: (layer, 0, 0)),
        pl.BlockSpec((1, CONV_B, D_B), lambda i: (layer, 0, 0)),
        _layer_vec_spec(layer, D_B), _layer_vec_spec(layer, D_B), _layer_vec_spec(layer, D_B),
    ]
    args = [proj, conv_a_w, conv_b_w, conv_b_b, ln_g, ln_b]
    aliases = {}
    if prev_out is not None:
        in_specs.append(pl.BlockSpec(memory_space=pl.ANY))
        args.append(prev_out)
        aliases = {len(args) - 1: 0}
    return pl.pallas_call(
        functools.partial(_conv_kernel, seq_len=seq_len),
        grid=(STREAM_TILES,),
        in_specs=in_specs,
        out_specs=pl.BlockSpec((TILE, D_A + D_B), lambda i: (i + tile0, 0)),
        out_shape=jax.ShapeDtypeStruct((ROWS, D_A + D_B), BF16),
        scratch_shapes=[pltpu.VMEM((u_rows, D_B), F32), pltpu.VMEM((TILE, D_B), F32)],
        input_output_aliases=aliases,
        compiler_params=_params("arbitrary"),
        name="conv_mix",
    )(*args)


HALF_C = 1280
CHAINS = 8


def _ref_rows(c, m, ref_pos, pos_in_block):
    size = 2 * m
    if size >= 8:
        pieces = [jnp.broadcast_to(c[b * size + ref_pos:b * size + ref_pos + 1, :], (size, HEAD))
                  for b in range(CHUNK // size)]
        return pieces[0] if len(pieces) == 1 else jnp.concatenate(pieces, axis=0)
    if size == 4:
        c3 = c.reshape(CHUNK // 8, 8, HEAD)
        sub = lax.broadcasted_iota(jnp.int32, c3.shape, 1)
        lo = jnp.broadcast_to(c3[:, ref_pos:ref_pos + 1, :], c3.shape)
        hi = jnp.broadcast_to(c3[:, 4 + ref_pos:5 + ref_pos, :], c3.shape)
        return jnp.where(sub < 4, lo, hi).reshape(CHUNK, HEAD)
    out = c
    for p in range(size):
        shift = p - ref_pos
        if shift == 0:
            continue
        out = jnp.where(pos_in_block == p, pltpu.roll(c, shift % CHUNK, axis=0), out)
    return out


def _hgrn_chunk(q, v_bf, x, lb, s_t, consts, backward):
    tri2, row, xor_rc = consts
    sg = _sigmoid(x)
    one_m = 1.0 - lb
    lf = jnp.log2(lb + one_m * sg)
    kk = one_m * (1.0 - sg)
    lf_hi = lf.astype(BF16)
    lf_lo = (lf - lf_hi.astype(F32)).astype(BF16)
    yield
    cb = _dot(tri2, jnp.concatenate([lf_hi, lf_lo], axis=0))
    total = cb[CHUNK - 1:CHUNK, :]
    c = cb - lf if backward else cb
    scores = _dot_nt(q.astype(BF16), kk.astype(BF16))
    yield

    for b in range(N_LEVELS):
        m = 1 << b
        if m < 8:
            g = _ref_rows(c, m, m if backward else m - 1, row & (2 * m - 1))
            e = jnp.exp2(-jnp.abs(c - g))
            qt, kt = (q * e).astype(BF16), (kk * e).astype(BF16)
        else:
            zeros = jnp.zeros((m, HEAD), F32)
            q_rows, k_rows = [], []
            for blk in range(CHUNK // (2 * m)):
                first = slice(2 * m * blk, 2 * m * blk + m)
                second = slice(2 * m * blk + m, 2 * m * (blk + 1))
                if backward:
                    g = c[2 * m * blk + m:2 * m * blk + m + 1, :]
                    q_rows += [q[first] * jnp.exp2(g - c[first]), zeros]
                    k_rows += [zeros, kk[second] * jnp.exp2(c[second] - g)]
                else:
                    g = c[2 * m * blk + m - 1:2 * m * blk + m, :]
                    q_rows += [zeros, q[second] * jnp.exp2(c[second] - g)]
                    k_rows += [kk[first] * jnp.exp2(g - c[first]), zeros]
            qt = jnp.concatenate(q_rows, axis=0).astype(BF16)
            kt = jnp.concatenate(k_rows, axis=0).astype(BF16)
        yield
        scores = jnp.where(xor_rc < m, scores, _dot_nt(qt, kt))
        yield
    col = row ^ xor_rc
    scores = jnp.where((row <= col) if backward else (row >= col), scores, 0.0).astype(BF16)

    if backward:
        q_dec = q * jnp.exp2(total - c)
        k_dec = kk * jnp.exp2(c)
    else:
        q_dec = q * jnp.exp2(c)
        k_dec = kk * jnp.exp2(total - c)
    q_dec, k_dec = q_dec.astype(BF16), k_dec.astype(BF16)
    yield
    o = _dot(scores, v_bf) + _dot_nt(q_dec, s_t.astype(BF16))
    s_new = s_t * jnp.exp2(total) + _dot_tn(v_bf, k_dec)
    return o, s_new


def _interleave(gens):
    results = [None] * len(gens)
    live = list(range(len(gens)))
    while live:
        still = []
        for i in live:
            try:
                next(gens[i])
                still.append(i)
            except StopIteration as stop:
                results[i] = stop.value
        live = still
    return results


def _hgrn_kernel(*refs, n_seq, has_state_in, has_prev_out):
    lo_ref, hi_ref, lb_ref, gn_ref = refs[:4]
    k = 4
    s0_ref = None
    if has_state_in:
        s0_ref = refs[k]
        k += 1
    k += int(has_prev_out)
    y_ref = refs[k]
    k += 1
    sout_ref = None
    if not has_state_in:
        sout_ref = refs[k]
        k += 1
    q_scr, of_scr, ob_scr, s_scr = refs[k:k + 4]

    def section(idx, h):
        off = idx * D_C + h * HEAD
        return (lo_ref, off) if off < HALF_C else (hi_ref, off - HALF_C)

    n_chunks = TILE // CHUNK
    cps = n_chunks // n_seq
    q_scr[...] = _silu(lo_ref[:, 0:D_C]) * (HEAD ** -0.5)

    r_io = lax.broadcasted_iota(jnp.int32, (CHUNK, CHUNK), 0)
    c_io = lax.broadcasted_iota(jnp.int32, (CHUNK, CHUNK), 1)
    tri = jnp.where(c_io <= r_io, 1.0, 0.0).astype(BF16)
    consts = (jnp.concatenate([tri, tri], axis=1), r_io, r_io ^ c_io)

    for h in range(N_HEADS_C):
        for d in range(2):
            if has_state_in:
                s_scr[2 * h + d] = s0_ref[0, 0, d, h].T
            else:
                s_scr[2 * h + d] = jnp.zeros((HEAD, HEAD), F32)

    def body(it, carry):
        s = it // cps
        j = it % cps
        chains = [(h, d) for h in range(N_HEADS_C) for d in range(2)]
        for g0 in range(0, len(chains), CHAINS):
            group = chains[g0:g0 + CHAINS]
            gens, rows_of = [], []
            for h, d in group:
                backward = d == 1
                ch = s * cps + ((cps - 1 - j) if backward else j)
                rows = pl.ds(pl.multiple_of(ch * CHUNK, CHUNK), CHUNK)
                i_ref, i_off = section(1, h)
                f_ref, f_off = section(2 + d, h)
                s_t = s_scr[2 * h + d]
                if not has_state_in:
                    s_t = jnp.where(j == 0, 0.0, s_t)
                gens.append(_hgrn_chunk(q_scr[rows, h * HEAD:(h + 1) * HEAD],
                                        i_ref[rows, i_off:i_off + HEAD].astype(BF16),
                                        f_ref[rows, f_off:f_off + HEAD],
                                        lb_ref[d, 0, :, h * HEAD:(h + 1) * HEAD], s_t, consts, backward))
                rows_of.append(rows)
            for (h, d), rows, (o, s_new) in zip(group, rows_of, _interleave(gens)):
                (ob_scr if d == 1 else of_scr)[rows, h * HEAD:(h + 1) * HEAD] = o
                s_scr[2 * h + d] = s_new
                if sout_ref is not None:
                    sout_ref[s, d, h] = s_new.T
        return carry

    lax.fori_loop(0, n_chunks, body, 0)

    for h in range(N_HEADS_C):
        cols = slice(h * HEAD, (h + 1) * HEAD)
        g_ref, g_off = section(4, h)
        o = of_scr[:, cols] + ob_scr[:, cols]
        y = _rms(o) * gn_ref[0, :, cols] * _silu(g_ref[:, g_off:g_off + HEAD])
        y_ref[:, cols] = y.astype(y_ref.dtype)


def _hgrn_mix(proj, lower_bound, gn_g, state_in, layer, n_seq, tile0, prev_out):
    has_in = state_in is not None
    blk0 = (3 * D_A + 2 * D_B) // HALF_C
    in_specs = [
        pl.BlockSpec((TILE, HALF_C), lambda i: (i + tile0, blk0)),
        pl.BlockSpec((TILE, HALF_C), lambda i: (i + tile0, blk0 + 1)),
        pl.BlockSpec((2, 1, 1, D_C), lambda i: (0, layer, 0, 0)),
        _layer_vec_spec(layer, D_C),
    ]
    args = [proj, proj, lower_bound, gn_g]
    if has_in:
        in_specs.append(pl.BlockSpec((1, 1, 2, N_HEADS_C, HEAD, HEAD), lambda i: (i, layer, 0, 0, 0, 0)))
        args.append(state_in)
    aliases = {}
    if prev_out is not None:
        in_specs.append(pl.BlockSpec(memory_space=pl.ANY))
        args.append(prev_out)
        aliases = {len(args) - 1: 0}
    out_specs = [pl.BlockSpec((TILE, D_C), lambda i: (i + tile0, 0))]
    out_shape = [jax.ShapeDtypeStruct((ROWS, D_C), BF16)]
    if not has_in:
        out_specs.append(pl.BlockSpec((n_seq, 2, N_HEADS_C, HEAD, HEAD), lambda i: (i, 0, 0, 0, 0)))
        out_shape.append(jax.ShapeDtypeStruct((STREAM_TILES * n_seq, 2, N_HEADS_C, HEAD, HEAD), F32))
    res = pl.pallas_call(
        functools.partial(_hgrn_kernel, n_seq=n_seq, has_state_in=has_in, has_prev_out=prev_out is not None),
        grid=(STREAM_TILES,),
        in_specs=in_specs,
        out_specs=out_specs,
        out_shape=out_shape,
        scratch_shapes=[pltpu.VMEM((TILE, D_C), F32), pltpu.VMEM((TILE, D_C), F32),
                        pltpu.VMEM((TILE, D_C), F32), pltpu.VMEM((2 * N_HEADS_C, HEAD, HEAD), F32)],
        input_output_aliases=aliases,
        compiler_params=_params("arbitrary"),
        name="hgrn_mix",
    )(*args)
    return (res[0], None) if has_in else (res[0], res[1])


ROUTER_LANES = 128
SUB = 128
N_SLAB = TILE // 128
OUT_ROWS = 256


def _route(lg, bias_ref):
    shape = lg[0].shape
    mx = functools.reduce(jnp.maximum, lg)
    ex = [jnp.exp(l - mx) for l in lg]
    den = functools.reduce(jnp.add, ex)
    sc = [e / den for e in ex]
    biased = [sc[e] + bias_ref[e:e + 1, :] for e in range(N_EXPERTS)]

    best = jnp.zeros(shape, jnp.int32)
    best_val = None
    for g in range(N_GROUPS):
        mem = biased[g * EXPERTS_PER_GROUP:(g + 1) * EXPERTS_PER_GROUP]
        pair = None
        for a in range(EXPERTS_PER_GROUP):
            for b in range(a + 1, EXPERTS_PER_GROUP):
                sm = mem[a] + mem[b]
                pair = sm if pair is None else jnp.maximum(pair, sm)
        if best_val is None:
            best_val = pair
        else:
            upd = pair > best_val
            best = jnp.where(upd, g, best)
            best_val = jnp.where(upd, pair, best_val)

    masked = [jnp.where(best == (e // EXPERTS_PER_GROUP), biased[e], MASK_VALUE) for e in range(N_EXPERTS)]
    v1, i1 = masked[0], jnp.zeros(shape, jnp.int32)
    for e in range(1, N_EXPERTS):
        upd = masked[e] > v1
        i1 = jnp.where(upd, e, i1)
        v1 = jnp.where(upd, masked[e], v1)
    v2 = jnp.full(shape, -jnp.inf, F32)
    i2 = jnp.full(shape, -1, jnp.int32)
    for e in range(N_EXPERTS):
        cand = jnp.where(i1 == e, -jnp.inf, masked[e])
        upd = cand > v2
        i2 = jnp.where(upd, e, i2)
        v2 = jnp.where(upd, cand, v2)
    w1 = functools.reduce(jnp.add, [jnp.where(i1 == e, sc[e], 0.0) for e in range(N_EXPERTS)])
    w2 = functools.reduce(jnp.add, [jnp.where(i2 == e, sc[e], 0.0) for e in range(N_EXPERTS)])
    inv = 1.0 / (w1 + w2)
    gates = [(jnp.where(i1 == e, w1, 0.0) + jnp.where(i2 == e, w2, 0.0)) * inv for e in range(N_EXPERTS)]
    return gates, best


def _plan(best):
    r_io = lax.broadcasted_iota(jnp.int32, (128, 128), 0)
    c_io = lax.broadcasted_iota(jnp.int32, (128, 128), 1)
    before = jnp.where(r_io < c_io, 1.0, 0.0).astype(BF16)
    ranks, counts = [], []
    for g in range(N_GROUPS):
        member = jnp.where(best == g, 1.0, 0.0)
        within = _dot(member.astype(BF16), before)
        tot = jnp.sum(member, axis=1, keepdims=True)
        run = jnp.zeros((1, 1), F32)
        offs = []
        for c in range(N_SLAB):
            offs.append(run)
            run = run + tot[c:c + 1]
        ranks.append((member, within + jnp.concatenate(offs, axis=0)))
        counts.append(run)
    subs = [jnp.floor((n + (SUB - 1.0)) * (1.0 / SUB)) for n in counts]
    start = jnp.zeros((1, 1), F32)
    starts = []
    pos = jnp.zeros(best.shape, F32)
    for g in range(N_GROUPS):
        starts.append(start)
        member, rank = ranks[g]
        pos = pos + member * (start * SUB + rank)
        start = start + subs[g]
    return pos.astype(jnp.int32), jnp.concatenate(starts + subs, axis=0)


def _outproj_chunk(rows, x_ref, yab_ref, yc_ref, w_ref, xo_ref, h_ref, mod, g_norm, wr_split):
    g1, sh2, sc2 = mod
    y = _dot(yab_ref[rows, :], w_ref[0, 0:D_A + D_B, :]) + _dot(yc_ref[rows, :], w_ref[0, D_A + D_B:, :])
    yield
    x = x_ref[rows, :] + g1 * y
    xo_ref[rows, :] = x
    h = _rms(x) * g_norm * (1.0 + sc2) + sh2
    h_hi = h.astype(BF16)
    h_ref[rows, :] = h_hi
    h_lo = (h - h_hi.astype(F32)).astype(BF16)
    yield
    prod = _dot(jnp.concatenate([h_hi, h_lo], axis=0), wr_split)
    logits = (prod[0:OUT_ROWS, 0:ROUTER_LANES] + prod[0:OUT_ROWS, ROUTER_LANES:]
              + prod[OUT_ROWS:, 0:ROUTER_LANES])
    yield
    return logits.T


def _outproj_kernel(x_ref, yab_ref, yc_ref, mod_ref, g_ref, w_ref, wr_ref, br_ref, xo_ref, h_ref, gates_ref,
                    pos_ref, tab_ref):
    m = mod_ref[0, 0]
    mod = (m[:, 2 * D_MODEL:3 * D_MODEL], m[:, 3 * D_MODEL:4 * D_MODEL], m[:, 4 * D_MODEL:5 * D_MODEL])
    wr = wr_ref[...]
    wr_hi = wr.astype(BF16)
    wr_lo = (wr - wr_hi.astype(F32)).astype(BF16)
    wr_split = jnp.concatenate([wr_hi, wr_lo], axis=1)
    chunks = [_outproj_chunk(pl.ds(c * OUT_ROWS, OUT_ROWS), x_ref, yab_ref, yc_ref, w_ref, xo_ref, h_ref, mod,
                             g_ref[0], wr_split) for c in range(TILE // OUT_ROWS)]
    logits_t = _interleave(chunks)
    lg = [jnp.concatenate([lt[e:e + 1, k * 128:(k + 1) * 128] for lt in logits_t for k in range(OUT_ROWS // 128)],
                          axis=0) for e in range(N_EXPERTS)]
    gates, best = _route(lg, br_ref)
    for e in range(N_EXPERTS):
        gates_ref[0, e] = gates[e]
    pos, tab = _plan(best)
    pos_ref[0] = pos
    tab_ref[0] = jnp.broadcast_to(tab, (8, 128)).astype(jnp.int32)


def _outproj(x, y_ab, y_c, mod, norm_g, w_out_bf, w_router_pad, b_router, layer):
    n_blk = ROWS // TILE
    return pl.pallas_call(
        _outproj_kernel,
        grid=(n_blk,),
        in_specs=[
            pl.BlockSpec((TILE, D_MODEL), lambda i: (i, 0)),
            pl.BlockSpec((TILE, D_A + D_B), lambda i: (i, 0)),
            pl.BlockSpec((TILE, D_C), lambda i: (i, 0)),
            _mod_spec(layer, lambda i: i),
            _layer_vec_spec(layer, D_MODEL),
            pl.BlockSpec((1, D_MODEL, D_MODEL), lambda i: (layer, 0, 0)),
            pl.BlockSpec((D_MODEL, ROUTER_LANES), lambda i: (0, 0)),
            pl.BlockSpec((N_EXPERTS, 1), lambda i: (0, 0)),
        ],
        out_specs=[
            pl.BlockSpec((TILE, D_MODEL), lambda i: (i, 0)),
            pl.BlockSpec((TILE, D_MODEL), lambda i: (i, 0)),
            pl.BlockSpec((1, N_EXPERTS, N_SLAB, 128), lambda i: (i, 0, 0, 0)),
            pl.BlockSpec((1, N_SLAB, 128), lambda i: (i, 0, 0)),
            pl.BlockSpec((1, 8, 128), lambda i: (i, 0, 0)),
        ],
        out_shape=[
            jax.ShapeDtypeStruct((ROWS, D_MODEL), F32),
            jax.ShapeDtypeStruct((ROWS, D_MODEL), BF16),
            jax.ShapeDtypeStruct((n_blk, N_EXPERTS, N_SLAB, 128), F32),
            jax.ShapeDtypeStruct((n_blk, N_SLAB, 128), jnp.int32),
            jax.ShapeDtypeStruct((n_blk, 8, 128), jnp.int32),
        ],
        compiler_params=_params("arbitrary"),
        name="outproj_router",
    )(x, y_ab, y_c, mod, norm_g, w_out_bf, w_router_pad, b_router)


MOE_VMEM_LIMIT = 58 * 1024 * 1024


def _moe_kernel(tab_ref, x_ref, h_ref, pos_ref, gates_ref, mod_ref, wg_ref, wu_ref, wd_ref, fg_ref, out_ref, *, final):
    blk = pl.program_id(0)
    grp = pl.program_id(1)

    @pl.when(grp == 0)
    def _():
        out_ref[...] = jnp.zeros_like(out_ref)

    first = tab_ref[blk, grp]
    count = tab_ref[blk, N_GROUPS + grp]
    gates = gates_ref[0, 0]
    g_hi = gates.astype(BF16)
    g_lo = (gates - g_hi.astype(F32)).astype(BF16)
    g_split = jnp.concatenate([g_hi, g_lo], axis=0)

    def sorted_rows(s, n_sub):
        rows = n_sub * SUB
        r_io = lax.broadcasted_iota(jnp.int32, (rows, TILE), 0)
        sel = jnp.where(r_io == pos_ref[...] - (first + s) * SUB, 1.0, 0.0).astype(BF16)
        xs = _dot(sel, h_ref[...]).astype(BF16)
        gs = _dot_nt(sel, g_split)
        y = jnp.zeros((rows, D_MODEL), F32)
        for e in range(EXPERTS_PER_GROUP):
            gate = gs[:, e:e + 1] + gs[:, EXPERTS_PER_GROUP + e:EXPERTS_PER_GROUP + e + 1]
            act = _silu(_dot(xs, wg_ref[0, e])) * _dot(xs, wu_ref[0, e]) * gate
            y = y + _dot(act.astype(BF16), wd_ref[0, e])
        y_hi = y.astype(BF16)
        y_lo = (y - y_hi.astype(F32)).astype(BF16)
        out_ref[...] += _dot_tn(jnp.concatenate([sel, sel], axis=0), jnp.concatenate([y_hi, y_lo], axis=0))

    def pair(p, carry):
        sorted_rows(2 * p, 2)
        return carry

    lax.fori_loop(0, count // 2, pair, 0)

    @pl.when(count % 2 == 1)
    def _():
        sorted_rows(count - 1, 1)

    @pl.when(grp == N_GROUPS - 1)
    def _():
        g2 = mod_ref[0, 0][:, 5 * D_MODEL:6 * D_MODEL]
        x = x_ref[...] + g2 * out_ref[...]
        if final:
            x = _rms(x) * fg_ref[...]
        out_ref[...] = x


def _moe(x, h, pos, tab, gates, mod, wg_bf, wu_bf, wd_bf, final_g, layer, final):
    grid_spec = pltpu.PrefetchScalarGridSpec(
        num_scalar_prefetch=1,
        grid=(ROWS // TILE, N_GROUPS),
        in_specs=[
            pl.BlockSpec((TILE, D_MODEL), lambda i, g, t: (i, 0)),
            pl.BlockSpec((TILE, D_MODEL), lambda i, g, t: (i, 0)),
            pl.BlockSpec((1, TILE), lambda i, g, t: (0, i)),
            pl.BlockSpec((1, 1, EXPERTS_PER_GROUP, TILE), lambda i, g, t: (i, g, 0, 0)),
            pl.BlockSpec((1, 1, 1, N_MOD * D_MODEL), lambda i, g, t: (layer, _cond_of_tile(i), 0, 0)),
            pl.BlockSpec((1, EXPERTS_PER_GROUP, D_MODEL, D_EXPERT), lambda i, g, t: (layer, g, 0, 0)),
            pl.BlockSpec((1, EXPERTS_PER_GROUP, D_MODEL, D_EXPERT), lambda i, g, t: (layer, g, 0, 0)),
            pl.BlockSpec((1, EXPERTS_PER_GROUP, D_EXPERT, D_MODEL), lambda i, g, t: (layer, g, 0, 0)),
            pl.BlockSpec((1, D_MODEL), lambda i, g, t: (0, 0)),
        ],
        out_specs=pl.BlockSpec((TILE, D_MODEL), lambda i, g, t: (i, 0)),
    )
    return pl.pallas_call(
        functools.partial(_moe_kernel, final=final),
        grid_spec=grid_spec,
        out_shape=jax.ShapeDtypeStruct((ROWS, D_MODEL), F32),
        compiler_params=pltpu.CompilerParams(dimension_semantics=("arbitrary", "arbitrary"),
                                             vmem_limit_bytes=MOE_VMEM_LIMIT),
        name="moe",
    )(tab, x, h, pos, gates, mod, wg_bf, wu_bf, wd_bf, final_g)


def kernel(x_prompt, x_sample, state_hgrn, c, c_ctx, norm_mix_g, norm_ffn_g, final_norm_g, w_ada, b_ada, w_in, w_out, conv_a_w, conv_b_w, conv_b_b, ln_b_g, ln_b_b, lb_logits, gnorm_c_g, w_router, b_router, w_gate, w_up, w_down):
    batch, seq, _ = x_prompt.shape
    dec_batch, dec_seq, _ = x_sample.shape
    assert batch * seq == STREAM_ROWS and dec_batch * dec_seq == STREAM_ROWS
    assert dec_seq == TILE and TILE % seq == 0 and dec_batch + 1 <= 8

    vec = lambda a: a.reshape(DEPTH, 1, a.shape[-1])
    w_in_bf, w_out_bf = w_in.astype(BF16), w_out.astype(BF16)
    wg_bf, wu_bf, wd_bf = w_gate.astype(BF16), w_up.astype(BF16), w_down.astype(BF16)
    w_router_pad = jnp.pad(w_router, ((0, 0), (0, ROUTER_LANES - N_EXPERTS)))
    b_router2 = b_router.reshape(N_EXPERTS, 1)
    final_g = final_norm_g.reshape(1, D_MODEL)
    norm_mix, norm_ffn, gn_g = vec(norm_mix_g), vec(norm_ffn_g), vec(gnorm_c_g)
    cb_b, ln_g, ln_b = vec(conv_b_b), vec(ln_b_g), vec(ln_b_b)

    lower_bound = _lower_bound(lb_logits).reshape(2, DEPTH, 1, D_C)
    cond8 = jnp.concatenate([c_ctx[None, :], c, jnp.zeros((8 - 1 - dec_batch, D_MODEL), F32)], axis=0)
    mod = _adaln(cond8, w_ada, b_ada).reshape(DEPTH, 8, 1, N_MOD * D_MODEL)

    x = jnp.concatenate([x_prompt.reshape(STREAM_ROWS, D_MODEL), x_sample.reshape(STREAM_ROWS, D_MODEL)], axis=0)
    states = []
    for l in range(DEPTH):
        proj = _inproj(x, mod, norm_mix, w_in_bf, l)
        y_ab = _conv_mix(proj, conv_a_w, conv_b_w, cb_b, ln_g, ln_b, l, seq, 0, None)
        y_ab = _conv_mix(proj, conv_a_w, conv_b_w, cb_b, ln_g, ln_b, l, GRID_W, STREAM_TILES, y_ab)
        y_c, s_new = _hgrn_mix(proj, lower_bound, gn_g, None, l, TILE // seq, 0, None)
        y_c, _ = _hgrn_mix(proj, lower_bound, gn_g, state_hgrn, l, 1, STREAM_TILES, y_c)
        states.append(s_new)
        x, h, gates, pos, tab = _outproj(x, y_ab, y_c, mod, norm_ffn, w_out_bf, w_router_pad, b_router2, l)
        x = _moe(x, h, pos.reshape(1, ROWS), tab[:, :, 0],
                 gates.reshape(ROWS // TILE, N_GROUPS, EXPERTS_PER_GROUP, TILE), mod,
                 wg_bf, wu_bf, wd_bf, final_g, l, final=(l == DEPTH - 1))
    new_state = jnp.stack(states, axis=1)
    return (x[:STREAM_ROWS].reshape(batch, seq, D_MODEL), x[STREAM_ROWS:].reshape(dec_batch, dec_seq, D_MODEL),
            new_state)
```

```python
import functools

import jax
import jax.numpy as jnp
from jax import lax
from jax.experimental import pallas as pl
from jax.experimental.pallas import tpu as pltpu

F32 = jnp.float32
BF16 = jnp.bfloat16

D_MODEL = 1024
DEPTH = 4
GRID_W = 64
D_A = 256
D_B = 256
D_C = 512
N_HEADS_C = 4
HEAD = 128
CONV_A = 3
CONV_B = 31
N_EXPERTS = 16
N_GROUPS = 4
EXPERTS_PER_GROUP = 4
D_EXPERT = 512
N_MOD = 6
D_IN = 3840
EPS = 1e-6
MASK_VALUE = -1e9
LOG2E = 1.4426950408889634

STREAM_ROWS = 4096
ROWS = 2 * STREAM_ROWS
TILE = 1024
STREAM_TILES = STREAM_ROWS // TILE
CHUNK = 128
N_LEVELS = 7
VMEM_LIMIT = 52 * 1024 * 1024


def _params(*sem):
    return pltpu.CompilerParams(dimension_semantics=sem, vmem_limit_bytes=VMEM_LIMIT)


def _dot(a, b):
    return jnp.dot(a, b, preferred_element_type=F32)


def _dot_nt(a, b):
    return lax.dot_general(a, b, (((1,), (1,)), ((), ())), preferred_element_type=F32)


def _dot_tn(a, b):
    return lax.dot_general(a, b, (((0,), (0,)), ((), ())), preferred_element_type=F32)


def _sigmoid(x):
    return 1.0 / (1.0 + jnp.exp2(x * (-LOG2E)))


def _silu(x):
    return x * _sigmoid(x)


def _silu_tanh(x):
    return x * (0.5 * jnp.tanh(0.5 * x) + 0.5)


def _rms(x):
    return x * lax.rsqrt(jnp.mean(x * x, axis=-1, keepdims=True) + EPS)


def _cond_of_tile(i):
    return jnp.maximum(i - (STREAM_TILES - 1), 0)


def _lower_bound_kernel(lg_ref, out_ref):
    for d in range(2):
        x = lg_ref[d]
        mx = jnp.max(x, axis=0, keepdims=True)
        e = jnp.exp(x - mx)
        p = e / jnp.sum(e, axis=0, keepdims=True)
        run = jnp.zeros_like(p[0:1])
        rows = []
        for l in range(DEPTH):
            run = run + p[l:l + 1]
            rows.append(run - p[0:1])
        out_ref[d] = jnp.concatenate(rows, axis=0)


def _lower_bound(lb_logits):
    return pl.pallas_call(
        _lower_bound_kernel,
        out_shape=jax.ShapeDtypeStruct((2, DEPTH, D_C), F32),
        name="lower_bound",
    )(lb_logits)


def _adaln_kernel(cond_ref, w_ref, b_ref, out_ref):
    a = _silu(cond_ref[...]).astype(BF16)
    out_ref[0] = _dot(a, w_ref[0].astype(BF16)) + b_ref[0]


def _adaln(cond8, w_ada, b_ada):
    tn = 1536
    nt = (N_MOD * D_MODEL) // tn
    return pl.pallas_call(
        _adaln_kernel,
        grid=(DEPTH, nt),
        in_specs=[
            pl.BlockSpec((8, D_MODEL), lambda l, j: (0, 0)),
            pl.BlockSpec((1, D_MODEL, tn), lambda l, j: (l, 0, j)),
            pl.BlockSpec((1, 1, tn), lambda l, j: (l, 0, j)),
        ],
        out_specs=pl.BlockSpec((1, 8, tn), lambda l, j: (l, 0, j)),
        out_shape=jax.ShapeDtypeStruct((DEPTH, 8, N_MOD * D_MODEL), F32),
        compiler_params=_params("arbitrary", "arbitrary"),
        name="adaln",
    )(cond8, w_ada, b_ada.reshape(DEPTH, 1, N_MOD * D_MODEL))


def _mod_spec(layer, tile_of):
    return pl.BlockSpec((1, 1, 1, N_MOD * D_MODEL),
                        lambda *g: (layer, _cond_of_tile(tile_of(*g)), 0, 0))


def _layer_vec_spec(layer, width):
    return pl.BlockSpec((1, 1, width), lambda *g: (layer, 0, 0))


IN_TN = 768


def _inproj_kernel(x_ref, mod_ref, g_ref, w_ref, out_ref, h_scr, w_scr):
    j = pl.program_id(0)
    i = pl.program_id(1)
    rows = pl.ds(pl.multiple_of(i * TILE, TILE), TILE)

    @pl.when(i == 0)
    def _():
        w_scr[...] = w_ref[0].astype(BF16)

    @pl.when(j == 0)
    def _():
        m = mod_ref[0, 0]
        h = _rms(x_ref[...]) * g_ref[0] * (1.0 + m[:, D_MODEL:2 * D_MODEL]) + m[:, 0:D_MODEL]
        h_scr[rows, :] = h.astype(BF16)

    out_ref[...] = _dot(h_scr[rows, :], w_scr[...])


def _inproj(x, mod, norm_g, w_in, layer):
    last = ROWS // TILE - 1

    def x_tile(j, i):
        return jnp.where(j == 0, i, last)

    return pl.pallas_call(
        _inproj_kernel,
        grid=(D_IN // IN_TN, ROWS // TILE),
        in_specs=[
            pl.BlockSpec((TILE, D_MODEL), lambda j, i: (x_tile(j, i), 0)),
            _mod_spec(layer, x_tile),
            _layer_vec_spec(layer, D_MODEL),
            pl.BlockSpec((1, D_MODEL, IN_TN), lambda j, i: (layer, 0, j)),
        ],
        out_specs=pl.BlockSpec((TILE, IN_TN), lambda j, i: (i, j)),
        out_shape=jax.ShapeDtypeStruct((ROWS, D_IN), F32),
        scratch_shapes=[pltpu.VMEM((ROWS, D_MODEL), BF16), pltpu.VMEM((D_MODEL, IN_TN), BF16)],
        compiler_params=_params("arbitrary", "arbitrary"),
        name="inproj",
    )(x, mod, norm_g, w_in)


PAD_B = 16
CONV_ROWS = 64


def _conv_kernel(p_ref, wa_ref, wb_ref, bb_ref, lng_ref, lnb_ref, *rest, seq_len):
    latent = seq_len == GRID_W
    if latent:
        out_ref, u_scr, c_scr = rest[-3:]
    else:
        out_ref, u_scr, c_scr, sh_scr = rest[-4:]
    a_b = p_ref[:, 0:D_A]
    z = p_ref[:, D_A:2 * D_A] * p_ref[:, 2 * D_A:3 * D_A]
    pos = lax.broadcasted_iota(jnp.int32, (TILE, D_A), 0) % seq_len
    z_prev = jnp.where(pos == 0, 0.0, pltpu.roll(z, 1, axis=0))
    z_next = jnp.where(pos == seq_len - 1, 0.0, pltpu.roll(z, TILE - 1, axis=0))
    wa = wa_ref[0]
    y_a = a_b * (wa[0:1] * z_prev + wa[1:2] * z + wa[2:3] * z_next)
    out_ref[:, 0:D_A] = y_a.astype(out_ref.dtype)

    u = p_ref[:, 3 * D_A:3 * D_A + D_B] * _sigmoid(p_ref[:, 3 * D_A + D_B:3 * D_A + 2 * D_B])
    if latent:
        n_r = TILE // GRID_W
        u_scr[...] = u

        def row_body(r, carry):
            acc = jnp.zeros((GRID_W, D_B), F32)
            for rp in range(n_r):
                w = wb_ref[0, pl.ds(CONV_B // 2 + rp - r, 1), :]
                acc = acc + w * u_scr[rp * GRID_W:(rp + 1) * GRID_W, :]
            c_scr[pl.ds(pl.multiple_of(r * GRID_W, GRID_W), GRID_W), :] = acc
            return carry

        lax.fori_loop(0, n_r, row_body, 0)
    else:
        n_seq = TILE // seq_len
        padded = seq_len + 2 * PAD_B
        zeros = jnp.zeros((PAD_B, D_B), F32)
        for s in range(n_seq):
            u_scr[s * padded:s * padded + PAD_B, :] = zeros
            u_scr[s * padded + PAD_B:s * padded + PAD_B + seq_len, :] = u[s * seq_len:(s + 1) * seq_len]
            u_scr[s * padded + PAD_B + seq_len:(s + 1) * padded, :] = zeros
        u_scr[n_seq * padded:n_seq * padded + 8, :] = jnp.zeros((8, D_B), F32)
        for b in range(1, 8):
            sh_scr[b - 1] = u_scr[b:b + n_seq * padded, :]
        for s in range(n_seq):
            for c in range(seq_len // CONV_ROWS):
                base = s * padded + c * CONV_ROWS
                acc = jnp.zeros((CONV_ROWS, D_B), F32)
                for j in range(CONV_B):
                    off = j + PAD_B - CONV_B // 2
                    lo = base + off - off % 8
                    if off % 8 == 0:
                        tap = u_scr[lo:lo + CONV_ROWS, :]
                    else:
                        tap = sh_scr[off % 8 - 1, lo:lo + CONV_ROWS, :]
                    acc = acc + wb_ref[0, j:j + 1, :] * tap
                c_scr[s * seq_len + c * CONV_ROWS:s * seq_len + (c + 1) * CONV_ROWS, :] = acc

    v = c_scr[...] + bb_ref[0]
    mu = jnp.mean(v, axis=-1, keepdims=True)
    d = v - mu
    var = jnp.mean(d * d, axis=-1, keepdims=True)
    ln = d * lax.rsqrt(var + EPS) * lng_ref[0] + lnb_ref[0]
    out_ref[:, D_A:D_A + D_B] = _silu(ln).astype(out_ref.dtype)


def _conv_mix(proj, conv_a_w, conv_b_w, conv_b_b, ln_g, ln_b, layer, seq_len, tile0, prev_out):
    n_seq = TILE // seq_len
    scratch = [pltpu.VMEM((TILE, D_B), F32), pltpu.VMEM((TILE, D_B), F32)]
    if seq_len != GRID_W:
        pad_rows = n_seq * (seq_len + 2 * PAD_B)
        scratch = [pltpu.VMEM((pad_rows + 8, D_B), F32), pltpu.VMEM((TILE, D_B), F32),
                   pltpu.VMEM((7, pad_rows, D_B), F32)]
    wcols = 3 * D_A + 2 * D_B
    in_specs = [
        pl.BlockSpec((TILE, wcols), lambda i: (i + tile0, 0)),
        pl.BlockSpec((1, CONV_A, D_A), lambda i: (layer, 0, 0)),
        pl.BlockSpec((1, CONV_B, D_B), lambda i: (layer, 0, 0)),
        _layer_vec_spec(layer, D_B), _layer_vec_spec(layer, D_B), _layer_vec_spec(layer, D_B),
    ]
    args = [proj, conv_a_w, conv_b_w, conv_b_b, ln_g, ln_b]
    aliases = {}
    if prev_out is not None:
        in_specs.append(pl.BlockSpec(memory_space=pl.ANY))
        args.append(prev_out)
        aliases = {len(args) - 1: 0}
    return pl.pallas_call(
        functools.partial(_conv_kernel, seq_len=seq_len),
        grid=(STREAM_TILES,),
        in_specs=in_specs,
        out_specs=pl.BlockSpec((TILE, D_A + D_B), lambda i: (i + tile0, 0)),
        out_shape=jax.ShapeDtypeStruct((ROWS, D_A + D_B), BF16),
        scratch_shapes=scratch,
        input_output_aliases=aliases,
        compiler_params=_params("arbitrary"),
        name="conv_mix",
    )(*args)


HALF_C = 1280
CHAINS = 8


def _ref_rows(c, m, ref_pos, pos_in_block):
    size = 2 * m
    if size >= 8:
        pieces = [jnp.broadcast_to(c[b * size + ref_pos:b * size + ref_pos + 1, :], (size, HEAD))
                  for b in range(CHUNK // size)]
        return pieces[0] if len(pieces) == 1 else jnp.concatenate(pieces, axis=0)
    if size == 4:
        c3 = c.reshape(CHUNK // 8, 8, HEAD)
        sub = lax.broadcasted_iota(jnp.int32, c3.shape, 1)
        lo = jnp.broadcast_to(c3[:, ref_pos:ref_pos + 1, :], c3.shape)
        hi = jnp.broadcast_to(c3[:, 4 + ref_pos:5 + ref_pos, :], c3.shape)
        return jnp.where(sub < 4, lo, hi).reshape(CHUNK, HEAD)
    out = c
    for p in range(size):
        shift = p - ref_pos
        if shift == 0:
            continue
        out = jnp.where(pos_in_block == p, pltpu.roll(c, shift % CHUNK, axis=0), out)
    return out


def _hgrn_chunk(q, v_bf, x, lb, s_t, consts, backward):
    tri2, row, xor_rc = consts
    sg = _sigmoid(x)
    one_m = 1.0 - lb
    lf = jnp.log2(lb + one_m * sg)
    kk = one_m * (1.0 - sg)
    lf_hi = lf.astype(BF16)
    lf_lo = (lf - lf_hi.astype(F32)).astype(BF16)
    yield
    cb = _dot(tri2, jnp.concatenate([lf_hi, lf_lo], axis=0))
    total = cb[CHUNK - 1:CHUNK, :]
    c = cb - lf if backward else cb
    scores = _dot_nt(q.astype(BF16), kk.astype(BF16))
    yield

    for b in range(N_LEVELS):
        m = 1 << b
        if m < 8:
            g = _ref_rows(c, m, m if backward else m - 1, row & (2 * m - 1))
            e = jnp.exp2(-jnp.abs(c - g))
            qt, kt = (q * e).astype(BF16), (kk * e).astype(BF16)
        else:
            zeros = jnp.zeros((m, HEAD), F32)
            q_rows, k_rows = [], []
            for blk in range(CHUNK // (2 * m)):
                first = slice(2 * m * blk, 2 * m * blk + m)
                second = slice(2 * m * blk + m, 2 * m * (blk + 1))
                if backward:
                    g = c[2 * m * blk + m:2 * m * blk + m + 1, :]
                    q_rows += [q[first] * jnp.exp2(g - c[first]), zeros]
                    k_rows += [zeros, kk[second] * jnp.exp2(c[second] - g)]
                else:
                    g = c[2 * m * blk + m - 1:2 * m * blk + m, :]
                    q_rows += [zeros, q[second] * jnp.exp2(c[second] - g)]
                    k_rows += [kk[first] * jnp.exp2(g - c[first]), zeros]
            qt = jnp.concatenate(q_rows, axis=0).astype(BF16)
            kt = jnp.concatenate(k_rows, axis=0).astype(BF16)
        yield
        scores = jnp.where(xor_rc < m, scores, _dot_nt(qt, kt))
        yield
    col = row ^ xor_rc
    scores = jnp.where((row <= col) if backward else (row >= col), scores, 0.0).astype(BF16)

    if backward:
        q_dec = q * jnp.exp2(total - c)
        k_dec = kk * jnp.exp2(c)
    else:
        q_dec = q * jnp.exp2(c)
        k_dec = kk * jnp.exp2(total - c)
    q_dec, k_dec = q_dec.astype(BF16), k_dec.astype(BF16)
    yield
    o = _dot(scores, v_bf) + _dot_nt(q_dec, s_t.astype(BF16))
    s_new = s_t * jnp.exp2(total) + _dot_tn(v_bf, k_dec)
    return o, s_new


def _interleave(gens):
    results = [None] * len(gens)
    live = list(range(len(gens)))
    while live:
        still = []
        for i in live:
            try:
                next(gens[i])
                still.append(i)
            except StopIteration as stop:
                results[i] = stop.value
        live = still
    return results


def _hgrn_kernel(*refs, n_seq, has_state_in, n_aliased):
    lo_ref, hi_ref, lb_ref, gn_ref = refs[:4]
    k = 4
    s0_ref = None
    if has_state_in:
        s0_ref = refs[k]
        k += 1
    k += n_aliased
    y_ref = refs[k]
    k += 1
    sout_ref = None
    if not has_state_in:
        sout_ref = refs[k]
        k += 1
    q_scr, of_scr, ob_scr, s_scr = refs[k:k + 4]

    def section(idx, h):
        off = idx * D_C + h * HEAD
        return (lo_ref, off) if off < HALF_C else (hi_ref, off - HALF_C)

    n_chunks = TILE // CHUNK
    cps = n_chunks // n_seq
    q_scr[...] = _silu_tanh(lo_ref[:, 0:D_C]) * (HEAD ** -0.5)

    r_io = lax.broadcasted_iota(jnp.int32, (CHUNK, CHUNK), 0)
    c_io = lax.broadcasted_iota(jnp.int32, (CHUNK, CHUNK), 1)
    tri = jnp.where(c_io <= r_io, 1.0, 0.0).astype(BF16)
    consts = (jnp.concatenate([tri, tri], axis=1), r_io, r_io ^ c_io)

    for h in range(N_HEADS_C):
        for d in range(2):
            if has_state_in:
                s_scr[2 * h + d] = s0_ref[0, 0, d, h].T
            else:
                s_scr[2 * h + d] = jnp.zeros((HEAD, HEAD), F32)

    def body(it, carry):
        s = it // cps
        j = it % cps
        chains = [(h, d) for h in range(N_HEADS_C) for d in range(2)]
        for g0 in range(0, len(chains), CHAINS):
            group = chains[g0:g0 + CHAINS]
            gens, rows_of = [], []
            for h, d in group:
                backward = d == 1
                ch = s * cps + ((cps - 1 - j) if backward else j)
                rows = pl.ds(pl.multiple_of(ch * CHUNK, CHUNK), CHUNK)
                i_ref, i_off = section(1, h)
                f_ref, f_off = section(2 + d, h)
                s_t = s_scr[2 * h + d]
                if not has_state_in:
                    s_t = jnp.where(j == 0, 0.0, s_t)
                gens.append(_hgrn_chunk(q_scr[rows, h * HEAD:(h + 1) * HEAD],
                                        i_ref[rows, i_off:i_off + HEAD].astype(BF16),
                                        f_ref[rows, f_off:f_off + HEAD],
                                        lb_ref[d, 0, :, h * HEAD:(h + 1) * HEAD], s_t, consts, backward))
                rows_of.append(rows)
            for (h, d), rows, (o, s_new) in zip(group, rows_of, _interleave(gens)):
                (ob_scr if d == 1 else of_scr)[rows, h * HEAD:(h + 1) * HEAD] = o
                s_scr[2 * h + d] = s_new
                if sout_ref is not None:
                    sout_ref[s, 0, d, h] = s_new.T
        return carry

    lax.fori_loop(0, n_chunks, body, 0)

    for h in range(N_HEADS_C):
        cols = slice(h * HEAD, (h + 1) * HEAD)
        g_ref, g_off = section(4, h)
        o = of_scr[:, cols] + ob_scr[:, cols]
        y = _rms(o) * gn_ref[0, :, cols] * _silu_tanh(g_ref[:, g_off:g_off + HEAD])
        y_ref[:, cols] = y.astype(y_ref.dtype)


def _hgrn_mix(proj, lower_bound, gn_g, state_in, layer, n_seq, tile0, prev_out, prev_states=None):
    has_in = state_in is not None
    blk0 = (3 * D_A + 2 * D_B) // HALF_C
    in_specs = [
        pl.BlockSpec((TILE, HALF_C), lambda i: (i + tile0, blk0)),
        pl.BlockSpec((TILE, HALF_C), lambda i: (i + tile0, blk0 + 1)),
        pl.BlockSpec((2, 1, 1, D_C), lambda i: (0, layer, 0, 0)),
        _layer_vec_spec(layer, D_C),
    ]
    args = [proj, proj, lower_bound, gn_g]
    if has_in:
        in_specs.append(pl.BlockSpec((1, 1, 2, N_HEADS_C, HEAD, HEAD), lambda i: (i, layer, 0, 0, 0, 0)))
        args.append(state_in)
    aliases = {}
    for out_idx, prev in ((0, prev_out), (1, prev_states)):
        if prev is not None:
            in_specs.append(pl.BlockSpec(memory_space=pl.ANY))
            args.append(prev)
            aliases[len(args) - 1] = out_idx
    out_specs = [pl.BlockSpec((TILE, D_C), lambda i: (i + tile0, 0))]
    out_shape = [jax.ShapeDtypeStruct((ROWS, D_C), BF16)]
    if not has_in:
        out_specs.append(pl.BlockSpec((n_seq, 1, 2, N_HEADS_C, HEAD, HEAD), lambda i: (i, layer, 0, 0, 0, 0)))
        out_shape.append(jax.ShapeDtypeStruct((STREAM_TILES * n_seq, DEPTH, 2, N_HEADS_C, HEAD, HEAD), F32))
    res = pl.pallas_call(
        functools.partial(_hgrn_kernel, n_seq=n_seq, has_state_in=has_in, n_aliased=len(aliases)),
        grid=(STREAM_TILES,),
        in_specs=in_specs,
        out_specs=out_specs,
        out_shape=out_shape,
        scratch_shapes=[pltpu.VMEM((TILE, D_C), F32), pltpu.VMEM((TILE, D_C), F32),
                        pltpu.VMEM((TILE, D_C), F32), pltpu.VMEM((2 * N_HEADS_C, HEAD, HEAD), F32)],
        input_output_aliases=aliases,
        compiler_params=_params("arbitrary"),
        name="hgrn_mix",
    )(*args)
    return (res[0], None) if has_in else (res[0], res[1])


ROUTER_LANES = 128
SUB = 128
N_SLAB = TILE // 128
OUT_ROWS = 256


def _route(lg, bias_ref):
    shape = lg[0].shape
    mx = functools.reduce(jnp.maximum, lg)
    ex = [jnp.exp(l - mx) for l in lg]
    den = functools.reduce(jnp.add, ex)
    sc = [e / den for e in ex]
    biased = [sc[e] + bias_ref[e:e + 1, :] for e in range(N_EXPERTS)]

    best = jnp.zeros(shape, jnp.int32)
    best_val = None
    for g in range(N_GROUPS):
        mem = biased[g * EXPERTS_PER_GROUP:(g + 1) * EXPERTS_PER_GROUP]
        pair = None
        for a in range(EXPERTS_PER_GROUP):
            for b in range(a + 1, EXPERTS_PER_GROUP):
                sm = mem[a] + mem[b]
                pair = sm if pair is None else jnp.maximum(pair, sm)
        if best_val is None:
            best_val = pair
        else:
            upd = pair > best_val
            best = jnp.where(upd, g, best)
            best_val = jnp.where(upd, pair, best_val)

    masked = [jnp.where(best == (e // EXPERTS_PER_GROUP), biased[e], MASK_VALUE) for e in range(N_EXPERTS)]
    v1, i1 = masked[0], jnp.zeros(shape, jnp.int32)
    for e in range(1, N_EXPERTS):
        upd = masked[e] > v1
        i1 = jnp.where(upd, e, i1)
        v1 = jnp.where(upd, masked[e], v1)
    v2 = jnp.full(shape, -jnp.inf, F32)
    i2 = jnp.full(shape, -1, jnp.int32)
    for e in range(N_EXPERTS):
        cand = jnp.where(i1 == e, -jnp.inf, masked[e])
        upd = cand > v2
        i2 = jnp.where(upd, e, i2)
        v2 = jnp.where(upd, cand, v2)
    w1 = functools.reduce(jnp.add, [jnp.where(i1 == e, sc[e], 0.0) for e in range(N_EXPERTS)])
    w2 = functools.reduce(jnp.add, [jnp.where(i2 == e, sc[e], 0.0) for e in range(N_EXPERTS)])
    inv = 1.0 / (w1 + w2)
    gates = [(jnp.where(i1 == e, w1, 0.0) + jnp.where(i2 == e, w2, 0.0)) * inv for e in range(N_EXPERTS)]
    return gates, best


def _plan(best):
    r_io = lax.broadcasted_iota(jnp.int32, (128, 128), 0)
    c_io = lax.broadcasted_iota(jnp.int32, (128, 128), 1)
    before = jnp.where(r_io < c_io, 1.0, 0.0).astype(BF16)
    ranks, counts = [], []
    for g in range(N_GROUPS):
        member = jnp.where(best == g, 1.0, 0.0)
        within = _dot(member.astype(BF16), before)
        tot = jnp.sum(member, axis=1, keepdims=True)
        run = jnp.zeros((1, 1), F32)
        offs = []
        for c in range(N_SLAB):
            offs.append(run)
            run = run + tot[c:c + 1]
        ranks.append((member, within + jnp.concatenate(offs, axis=0)))
        counts.append(run)
    subs = [jnp.floor((n + (SUB - 1.0)) * (1.0 / SUB)) for n in counts]
    start = jnp.zeros((1, 1), F32)
    starts = []
    pos = jnp.zeros(best.shape, F32)
    for g in range(N_GROUPS):
        starts.append(start)
        member, rank = ranks[g]
        pos = pos + member * (start * SUB + rank)
        start = start + subs[g]
    return pos.astype(jnp.int32), jnp.concatenate(starts + subs, axis=0)


def _outproj_chunk(rows, x_ref, yab_ref, yc_ref, w_ref, xo_ref, h_ref, mod, g_norm, wr_split):
    g1, sh2, sc2 = mod
    y = _dot(yab_ref[rows, :], w_ref[0, 0:D_A + D_B, :]) + _dot(yc_ref[rows, :], w_ref[0, D_A + D_B:, :])
    yield
    x = x_ref[rows, :] + g1 * y
    xo_ref[rows, :] = x
    h = _rms(x) * g_norm * (1.0 + sc2) + sh2
    h_hi = h.astype(BF16)
    h_ref[rows, :] = h_hi
    h_lo = (h - h_hi.astype(F32)).astype(BF16)
    yield
    prod = _dot(jnp.concatenate([h_hi, h_lo], axis=0), wr_split)
    logits = (prod[0:OUT_ROWS, 0:ROUTER_LANES] + prod[0:OUT_ROWS, ROUTER_LANES:]
              + prod[OUT_ROWS:, 0:ROUTER_LANES])
    yield
    return logits.T


def _outproj_kernel(x_ref, yab_ref, yc_ref, mod_ref, g_ref, w_ref, wr_ref, br_ref, xo_ref, h_ref, gates_ref,
                    pos_ref, tab_ref):
    m = mod_ref[0, 0]
    mod = (m[:, 2 * D_MODEL:3 * D_MODEL], m[:, 3 * D_MODEL:4 * D_MODEL], m[:, 4 * D_MODEL:5 * D_MODEL])
    wr = wr_ref[...]
    wr_hi = wr.astype(BF16)
    wr_lo = (wr - wr_hi.astype(F32)).astype(BF16)
    wr_split = jnp.concatenate([wr_hi, wr_lo], axis=1)
    chunks = [_outproj_chunk(pl.ds(c * OUT_ROWS, OUT_ROWS), x_ref, yab_ref, yc_ref, w_ref, xo_ref, h_ref, mod,
                             g_ref[0], wr_split) for c in range(TILE // OUT_ROWS)]
    logits_t = _interleave(chunks)
    lg = [jnp.concatenate([lt[e:e + 1, k * 128:(k + 1) * 128] for lt in logits_t for k in range(OUT_ROWS // 128)],
                          axis=0) for e in range(N_EXPERTS)]
    gates, best = _route(lg, br_ref)
    for e in range(N_EXPERTS):
        gates_ref[0, e] = gates[e]
    pos, tab = _plan(best)
    pos_ref[0] = pos
    tab_ref[0] = jnp.broadcast_to(tab, (8, 128)).astype(jnp.int32)


def _outproj(x, y_ab, y_c, mod, norm_g, w_out_bf, w_router_pad, b_router, layer):
    n_blk = ROWS // TILE
    return pl.pallas_call(
        _outproj_kernel,
        grid=(n_blk,),
        in_specs=[
            pl.BlockSpec((TILE, D_MODEL), lambda i: (i, 0)),
            pl.BlockSpec((TILE, D_A + D_B), lambda i: (i, 0)),
            pl.BlockSpec((TILE, D_C), lambda i: (i, 0)),
            _mod_spec(layer, lambda i: i),
            _layer_vec_spec(layer, D_MODEL),
            pl.BlockSpec((1, D_MODEL, D_MODEL), lambda i: (layer, 0, 0)),
            pl.BlockSpec((D_MODEL, ROUTER_LANES), lambda i: (0, 0)),
            pl.BlockSpec((N_EXPERTS, 1), lambda i: (0, 0)),
        ],
        out_specs=[
            pl.BlockSpec((TILE, D_MODEL), lambda i: (i, 0)),
            pl.BlockSpec((TILE, D_MODEL), lambda i: (i, 0)),
            pl.BlockSpec((1, N_EXPERTS, N_SLAB, 128), lambda i: (i, 0, 0, 0)),
            pl.BlockSpec((1, N_SLAB, 128), lambda i: (i, 0, 0)),
            pl.BlockSpec((1, 8, 128), lambda i: (i, 0, 0)),
        ],
        out_shape=[
            jax.ShapeDtypeStruct((ROWS, D_MODEL), F32),
            jax.ShapeDtypeStruct((ROWS, D_MODEL), BF16),
            jax.ShapeDtypeStruct((n_blk, N_EXPERTS, N_SLAB, 128), F32),
            jax.ShapeDtypeStruct((n_blk, N_SLAB, 128), jnp.int32),
            jax.ShapeDtypeStruct((n_blk, 8, 128), jnp.int32),
        ],
        compiler_params=_params("arbitrary"),
        name="outproj_router",
    )(x, y_ab, y_c, mod, norm_g, w_out_bf, w_router_pad, b_router)


MOE_VMEM_LIMIT = 58 * 1024 * 1024


def _moe_kernel(tab_ref, x_ref, h_ref, pos_ref, gates_ref, mod_ref, wg_ref, wu_ref, wd_ref, fg_ref, out_ref, *, final):
    blk = pl.program_id(0)
    grp = pl.program_id(1)

    @pl.when(grp == 0)
    def _():
        out_ref[...] = jnp.zeros_like(out_ref)

    first = tab_ref[blk, grp]
    count = tab_ref[blk, N_GROUPS + grp]
    gates = gates_ref[0, 0]
    g_hi = gates.astype(BF16)
    g_lo = (gates - g_hi.astype(F32)).astype(BF16)
    g_split = jnp.concatenate([g_hi, g_lo], axis=0)

    def sorted_rows(s, n_sub):
        rows = n_sub * SUB
        r_io = lax.broadcasted_iota(jnp.int32, (rows, TILE), 0)
        sel = jnp.where(r_io == pos_ref[...] - (first + s) * SUB, 1.0, 0.0).astype(BF16)
        xs = _dot(sel, h_ref[...]).astype(BF16)
        gs = _dot_nt(sel, g_split)
        y = jnp.zeros((rows, D_MODEL), F32)
        for e in range(EXPERTS_PER_GROUP):
            gate = gs[:, e:e + 1] + gs[:, EXPERTS_PER_GROUP + e:EXPERTS_PER_GROUP + e + 1]
            act = _silu(_dot(xs, wg_ref[0, e])) * _dot(xs, wu_ref[0, e]) * gate
            y = y + _dot(act.astype(BF16), wd_ref[0, e])
        out_ref[...] += _dot_tn(sel, y.astype(BF16))

    def pair(p, carry):
        sorted_rows(2 * p, 2)
        return carry

    lax.fori_loop(0, count // 2, pair, 0)

    @pl.when(count % 2 == 1)
    def _():
        sorted_rows(count - 1, 1)

    @pl.when(grp == N_GROUPS - 1)
    def _():
        g2 = mod_ref[0, 0][:, 5 * D_MODEL:6 * D_MODEL]
        x = x_ref[...] + g2 * out_ref[...]
        if final:
            x = _rms(x) * fg_ref[...]
        out_ref[...] = x


def _moe(x, h, pos, tab, gates, mod, wg_bf, wu_bf, wd_bf, final_g, layer, final):
    grid_spec = pltpu.PrefetchScalarGridSpec(
        num_scalar_prefetch=1,
        grid=(ROWS // TILE, N_GROUPS),
        in_specs=[
            pl.BlockSpec((TILE, D_MODEL), lambda i, g, t: (i, 0)),
            pl.BlockSpec((TILE, D_MODEL), lambda i, g, t: (i, 0)),
            pl.BlockSpec((1, TILE), lambda i, g, t: (0, i)),
            pl.BlockSpec((1, 1, EXPERTS_PER_GROUP, TILE), lambda i, g, t: (i, g, 0, 0)),
            pl.BlockSpec((1, 1, 1, N_MOD * D_MODEL), lambda i, g, t: (layer, _cond_of_tile(i), 0, 0)),
            pl.BlockSpec((1, EXPERTS_PER_GROUP, D_MODEL, D_EXPERT), lambda i, g, t: (layer, g, 0, 0)),
            pl.BlockSpec((1, EXPERTS_PER_GROUP, D_MODEL, D_EXPERT), lambda i, g, t: (layer, g, 0, 0)),
            pl.BlockSpec((1, EXPERTS_PER_GROUP, D_EXPERT, D_MODEL), lambda i, g, t: (layer, g, 0, 0)),
            pl.BlockSpec((1, D_MODEL), lambda i, g, t: (0, 0)),
        ],
        out_specs=pl.BlockSpec((TILE, D_MODEL), lambda i, g, t: (i, 0)),
    )
    return pl.pallas_call(
        functools.partial(_moe_kernel, final=final),
        grid_spec=grid_spec,
        out_shape=jax.ShapeDtypeStruct((ROWS, D_MODEL), F32),
        compiler_params=pltpu.CompilerParams(dimension_semantics=("arbitrary", "arbitrary"),
                                             vmem_limit_bytes=MOE_VMEM_LIMIT),
        name="moe",
    )(tab, x, h, pos, gates, mod, wg_bf, wu_bf, wd_bf, final_g)


def kernel(x_prompt, x_sample, state_hgrn, c, c_ctx, norm_mix_g, norm_ffn_g, final_norm_g, w_ada, b_ada, w_in, w_out, conv_a_w, conv_b_w, conv_b_b, ln_b_g, ln_b_b, lb_logits, gnorm_c_g, w_router, b_router, w_gate, w_up, w_down):
    batch, seq, _ = x_prompt.shape
    dec_batch, dec_seq, _ = x_sample.shape
    assert batch * seq == STREAM_ROWS and dec_batch * dec_seq == STREAM_ROWS
    assert dec_seq == TILE and TILE % seq == 0 and dec_batch + 1 <= 8

    vec = lambda a: a.reshape(DEPTH, 1, a.shape[-1])
    w_out_bf = w_out.astype(BF16)
    wg_bf, wu_bf, wd_bf = w_gate.astype(BF16), w_up.astype(BF16), w_down.astype(BF16)
    w_router_pad = jnp.pad(w_router, ((0, 0), (0, ROUTER_LANES - N_EXPERTS)))
    b_router2 = b_router.reshape(N_EXPERTS, 1)
    final_g = final_norm_g.reshape(1, D_MODEL)
    norm_mix, norm_ffn, gn_g = vec(norm_mix_g), vec(norm_ffn_g), vec(gnorm_c_g)
    cb_b, ln_g, ln_b = vec(conv_b_b), vec(ln_b_g), vec(ln_b_b)

    lower_bound = _lower_bound(lb_logits).reshape(2, DEPTH, 1, D_C)
    cond8 = jnp.concatenate([c_ctx[None, :], c, jnp.zeros((8 - 1 - dec_batch, D_MODEL), F32)], axis=0)
    mod = _adaln(cond8, w_ada, b_ada).reshape(DEPTH, 8, 1, N_MOD * D_MODEL)

    x = jnp.concatenate([x_prompt.reshape(STREAM_ROWS, D_MODEL), x_sample.reshape(STREAM_ROWS, D_MODEL)], axis=0)
    states = None
    for l in range(DEPTH):
        proj = _inproj(x, mod, norm_mix, w_in, l)
        y_ab = _conv_mix(proj, conv_a_w, conv_b_w, cb_b, ln_g, ln_b, l, seq, 0, None)
        y_ab = _conv_mix(proj, conv_a_w, conv_b_w, cb_b, ln_g, ln_b, l, GRID_W, STREAM_TILES, y_ab)
        y_c, states = _hgrn_mix(proj, lower_bound, gn_g, None, l, TILE // seq, 0, None, states)
        y_c, _ = _hgrn_mix(proj, lower_bound, gn_g, state_hgrn, l, 1, STREAM_TILES, y_c)
        x, h, gates, pos, tab = _outproj(x, y_ab, y_c, mod, norm_ffn, w_out_bf, w_router_pad, b_router2, l)
        x = _moe(x, h, pos.reshape(1, ROWS), tab[:, :, 0],
                 gates.reshape(ROWS // TILE, N_GROUPS, EXPERTS_PER_GROUP, TILE), mod,
                 wg_bf, wu_bf, wd_bf, final_g, l, final=(l == DEPTH - 1))
    return (x[:STREAM_ROWS].reshape(batch, seq, D_MODEL), x[STREAM_ROWS:].reshape(dec_batch, dec_seq, D_MODEL),
            states)
```

```python
import functools

import jax
import jax.numpy as jnp
from jax import lax
from jax.experimental import pallas as pl
from jax.experimental.pallas import tpu as pltpu

F32 = jnp.float32
BF16 = jnp.bfloat16

D_MODEL = 1024
DEPTH = 4
GRID_W = 64
D_A = 256
D_B = 256
D_C = 512
N_HEADS_C = 4
HEAD = 128
CONV_A = 3
CONV_B = 31
N_EXPERTS = 16
N_GROUPS = 4
EXPERTS_PER_GROUP = 4
D_EXPERT = 512
N_MOD = 6
D_IN = 3840
EPS = 1e-6
MASK_VALUE = -1e9
LOG2E = 1.4426950408889634

STREAM_ROWS = 4096
ROWS = 2 * STREAM_ROWS
TILE = 1024
STREAM_TILES = STREAM_ROWS // TILE
CHUNK = 128
N_LEVELS = 7
VMEM_LIMIT = 52 * 1024 * 1024


def _params(*sem):
    return pltpu.CompilerParams(dimension_semantics=sem, vmem_limit_bytes=VMEM_LIMIT)


def _dot(a, b):
    return jnp.dot(a, b, preferred_element_type=F32)


def _dot_nt(a, b):
    return lax.dot_general(a, b, (((1,), (1,)), ((), ())), preferred_element_type=F32)


def _dot_tn(a, b):
    return lax.dot_general(a, b, (((0,), (0,)), ((), ())), preferred_element_type=F32)


def _sigmoid(x):
    return 1.0 / (1.0 + jnp.exp2(x * (-LOG2E)))


def _silu(x):
    return x * _sigmoid(x)


def _silu_tanh(x):
    return x * (0.5 * jnp.tanh(0.5 * x) + 0.5)


def _rms(x):
    return x * lax.rsqrt(jnp.mean(x * x, axis=-1, keepdims=True) + EPS)


def _cond_of_tile(i):
    return jnp.maximum(i - (STREAM_TILES - 1), 0)


def _lower_bound_kernel(lg_ref, out_ref):
    for d in range(2):
        x = lg_ref[d]
        mx = jnp.max(x, axis=0, keepdims=True)
        e = jnp.exp(x - mx)
        p = e / jnp.sum(e, axis=0, keepdims=True)
        run = jnp.zeros_like(p[0:1])
        rows = []
        for l in range(DEPTH):
            run = run + p[l:l + 1]
            rows.append(run - p[0:1])
        out_ref[d] = jnp.concatenate(rows, axis=0)


def _lower_bound(lb_logits):
    return pl.pallas_call(
        _lower_bound_kernel,
        out_shape=jax.ShapeDtypeStruct((2, DEPTH, D_C), F32),
        name="lower_bound",
    )(lb_logits)


def _adaln_kernel(cond_ref, w_ref, b_ref, out_ref):
    a = _silu(cond_ref[...]).astype(BF16)
    out_ref[0] = _dot(a, w_ref[0].astype(BF16)) + b_ref[0]


def _adaln(cond8, w_ada, b_ada):
    tn = 1536
    nt = (N_MOD * D_MODEL) // tn
    return pl.pallas_call(
        _adaln_kernel,
        grid=(DEPTH, nt),
        in_specs=[
            pl.BlockSpec((8, D_MODEL), lambda l, j: (0, 0)),
            pl.BlockSpec((1, D_MODEL, tn), lambda l, j: (l, 0, j)),
            pl.BlockSpec((1, 1, tn), lambda l, j: (l, 0, j)),
        ],
        out_specs=pl.BlockSpec((1, 8, tn), lambda l, j: (l, 0, j)),
        out_shape=jax.ShapeDtypeStruct((DEPTH, 8, N_MOD * D_MODEL), F32),
        compiler_params=_params("arbitrary", "arbitrary"),
        name="adaln",
    )(cond8, w_ada, b_ada.reshape(DEPTH, 1, N_MOD * D_MODEL))


def _mod_spec(layer, tile_of):
    return pl.BlockSpec((1, 1, 1, N_MOD * D_MODEL),
                        lambda *g: (layer, _cond_of_tile(tile_of(*g)), 0, 0))


def _layer_vec_spec(layer, width):
    return pl.BlockSpec((1, 1, width), lambda *g: (layer, 0, 0))


IN_TN = 768


def _stream_rows(x_refs, tile, rows=slice(None)):
    if len(x_refs) == 1:
        return x_refs[0][rows, :]
    return jnp.where(tile < STREAM_TILES, x_refs[0][rows, :], x_refs[1][rows, :])


def _stream_specs(x, tile_of):
    if not isinstance(x, tuple):
        return [pl.BlockSpec((TILE, D_MODEL), lambda *g: (tile_of(*g), 0))], [x]
    return ([pl.BlockSpec((TILE, D_MODEL), lambda *g: (jnp.minimum(tile_of(*g), STREAM_TILES - 1), 0)),
             pl.BlockSpec((TILE, D_MODEL), lambda *g: (jnp.maximum(tile_of(*g) - STREAM_TILES, 0), 0))],
            list(x))


def _inproj_kernel(*refs):
    mod_ref, g_ref, w_ref, out_ref, h_scr, w_scr = refs[-6:]
    x_refs = refs[:-6]
    j = pl.program_id(0)
    i = pl.program_id(1)
    rows = pl.ds(pl.multiple_of(i * TILE, TILE), TILE)

    @pl.when(i == 0)
    def _():
        w_scr[...] = w_ref[0].astype(BF16)

    @pl.when(j == 0)
    def _():
        m = mod_ref[0, 0]
        h = _rms(_stream_rows(x_refs, i)) * g_ref[0] * (1.0 + m[:, D_MODEL:2 * D_MODEL]) + m[:, 0:D_MODEL]
        h_scr[rows, :] = h.astype(BF16)

    out_ref[...] = _dot(h_scr[rows, :], w_scr[...])


def _inproj(x, mod, norm_g, w_in, layer):
    last = ROWS // TILE - 1

    def x_tile(j, i):
        return jnp.where(j == 0, i, last)

    x_specs, x_args = _stream_specs(x, x_tile)
    return pl.pallas_call(
        _inproj_kernel,
        grid=(D_IN // IN_TN, ROWS // TILE),
        in_specs=x_specs + [
            _mod_spec(layer, x_tile),
            _layer_vec_spec(layer, D_MODEL),
            pl.BlockSpec((1, D_MODEL, IN_TN), lambda j, i: (layer, 0, j)),
        ],
        out_specs=pl.BlockSpec((TILE, IN_TN), lambda j, i: (i, j)),
        out_shape=jax.ShapeDtypeStruct((ROWS, D_IN), F32),
        scratch_shapes=[pltpu.VMEM((ROWS, D_MODEL), BF16), pltpu.VMEM((D_MODEL, IN_TN), BF16)],
        compiler_params=_params("arbitrary", "arbitrary"),
        name="inproj",
    )(*x_args, mod, norm_g, w_in)


PAD_B = 16
CONV_ROWS = 64


def _conv_kernel(p_ref, wa_ref, wb_ref, bb_ref, lng_ref, lnb_ref, *rest, seq_len):
    latent = seq_len == GRID_W
    if latent:
        out_ref, u_scr, c_scr = rest[-3:]
    else:
        out_ref, u_scr, c_scr, sh_scr = rest[-4:]
    a_b = p_ref[:, 0:D_A]
    z = p_ref[:, D_A:2 * D_A] * p_ref[:, 2 * D_A:3 * D_A]
    pos = lax.broadcasted_iota(jnp.int32, (TILE, D_A), 0) % seq_len
    z_prev = jnp.where(pos == 0, 0.0, pltpu.roll(z, 1, axis=0))
    z_next = jnp.where(pos == seq_len - 1, 0.0, pltpu.roll(z, TILE - 1, axis=0))
    wa = wa_ref[0]
    y_a = a_b * (wa[0:1] * z_prev + wa[1:2] * z + wa[2:3] * z_next)
    out_ref[:, 0:D_A] = y_a.astype(out_ref.dtype)

    u = p_ref[:, 3 * D_A:3 * D_A + D_B] * _sigmoid(p_ref[:, 3 * D_A + D_B:3 * D_A + 2 * D_B])
    if latent:
        n_r = TILE // GRID_W
        u_scr[...] = u

        def row_body(r, carry):
            acc = jnp.zeros((GRID_W, D_B), F32)
            for rp in range(n_r):
                w = wb_ref[0, pl.ds(CONV_B // 2 + rp - r, 1), :]
                acc = acc + w * u_scr[rp * GRID_W:(rp + 1) * GRID_W, :]
            c_scr[pl.ds(pl.multiple_of(r * GRID_W, GRID_W), GRID_W), :] = acc
            return carry

        lax.fori_loop(0, n_r, row_body, 0)
    else:
        n_seq = TILE // seq_len
        padded = seq_len + 2 * PAD_B
        zeros = jnp.zeros((PAD_B, D_B), F32)
        for s in range(n_seq):
            u_scr[s * padded:s * padded + PAD_B, :] = zeros
            u_scr[s * padded + PAD_B:s * padded + PAD_B + seq_len, :] = u[s * seq_len:(s + 1) * seq_len]
            u_scr[s * padded + PAD_B + seq_len:(s + 1) * padded, :] = zeros
        u_scr[n_seq * padded:n_seq * padded + 8, :] = jnp.zeros((8, D_B), F32)
        for b in range(1, 8):
            sh_scr[b - 1] = u_scr[b:b + n_seq * padded, :]
        for s in range(n_seq):
            for c in range(seq_len // CONV_ROWS):
                base = s * padded + c * CONV_ROWS
                acc = jnp.zeros((CONV_ROWS, D_B), F32)
                for j in range(CONV_B):
                    off = j + PAD_B - CONV_B // 2
                    lo = base + off - off % 8
                    if off % 8 == 0:
                        tap = u_scr[lo:lo + CONV_ROWS, :]
                    else:
                        tap = sh_scr[off % 8 - 1, lo:lo + CONV_ROWS, :]
                    acc = acc + wb_ref[0, j:j + 1, :] * tap
                c_scr[s * seq_len + c * CONV_ROWS:s * seq_len + (c + 1) * CONV_ROWS, :] = acc

    v = c_scr[...] + bb_ref[0]
    mu = jnp.mean(v, axis=-1, keepdims=True)
    d = v - mu
    var = jnp.mean(d * d, axis=-1, keepdims=True)
    ln = d * lax.rsqrt(var + EPS) * lng_ref[0] + lnb_ref[0]
    out_ref[:, D_A:D_A + D_B] = _silu(ln).astype(out_ref.dtype)


def _conv_mix(proj, conv_a_w, conv_b_w, conv_b_b, ln_g, ln_b, layer, seq_len, tile0, prev_out):
    n_seq = TILE // seq_len
    scratch = [pltpu.VMEM((TILE, D_B), F32), pltpu.VMEM((TILE, D_B), F32)]
    if seq_len != GRID_W:
        pad_rows = n_seq * (seq_len + 2 * PAD_B)
        scratch = [pltpu.VMEM((pad_rows + 8, D_B), F32), pltpu.VMEM((TILE, D_B), F32),
                   pltpu.VMEM((7, pad_rows, D_B), F32)]
    wcols = 3 * D_A + 2 * D_B
    in_specs = [
        pl.BlockSpec((TILE, wcols), lambda i: (i + tile0, 0)),
        pl.BlockSpec((1, CONV_A, D_A), lambda i: (layer, 0, 0)),
        pl.BlockSpec((1, CONV_B, D_B), lambda i: (layer, 0, 0)),
        _layer_vec_spec(layer, D_B), _layer_vec_spec(layer, D_B), _layer_vec_spec(layer, D_B),
    ]
    args = [proj, conv_a_w, conv_b_w, conv_b_b, ln_g, ln_b]
    aliases = {}
    if prev_out is not None:
        in_specs.append(pl.BlockSpec(memory_space=pl.ANY))
        args.append(prev_out)
        aliases = {len(args) - 1: 0}
    return pl.pallas_call(
        functools.partial(_conv_kernel, seq_len=seq_len),
        grid=(STREAM_TILES,),
        in_specs=in_specs,
        out_specs=pl.BlockSpec((TILE, D_A + D_B), lambda i: (i + tile0, 0)),
        out_shape=jax.ShapeDtypeStruct((ROWS, D_A + D_B), BF16),
        scratch_shapes=scratch,
        input_output_aliases=aliases,
        compiler_params=_params("arbitrary"),
        name="conv_mix",
    )(*args)


HALF_C = 1280
CHAINS = 8


def _ref_rows(c, m, ref_pos, pos_in_block):
    size = 2 * m
    if size >= 8:
        pieces = [jnp.broadcast_to(c[b * size + ref_pos:b * size + ref_pos + 1, :], (size, HEAD))
                  for b in range(CHUNK // size)]
        return pieces[0] if len(pieces) == 1 else jnp.concatenate(pieces, axis=0)
    if size == 4:
        c3 = c.reshape(CHUNK // 8, 8, HEAD)
        sub = lax.broadcasted_iota(jnp.int32, c3.shape, 1)
        lo = jnp.broadcast_to(c3[:, ref_pos:ref_pos + 1, :], c3.shape)
        hi = jnp.broadcast_to(c3[:, 4 + ref_pos:5 + ref_pos, :], c3.shape)
        return jnp.where(sub < 4, lo, hi).reshape(CHUNK, HEAD)
    out = c
    for p in range(size):
        shift = p - ref_pos
        if shift == 0:
            continue
        out = jnp.where(pos_in_block == p, pltpu.roll(c, shift % CHUNK, axis=0), out)
    return out


def _hgrn_chunk(q, v_bf, x, lb, s_t, consts, backward):
    tri2, row, xor_rc = consts
    sg = _sigmoid(x)
    one_m = 1.0 - lb
    lf = jnp.log2(lb + one_m * sg)
    kk = one_m * (1.0 - sg)
    lf_hi = lf.astype(BF16)
    lf_lo = (lf - lf_hi.astype(F32)).astype(BF16)
    yield
    cb = _dot(tri2, jnp.concatenate([lf_hi, lf_lo], axis=0))
    total = cb[CHUNK - 1:CHUNK, :]
    c = cb - lf if backward else cb
    scores = _dot_nt(q.astype(BF16), kk.astype(BF16))
    yield

    for b in range(N_LEVELS):
        m = 1 << b
        if m < 8:
            g = _ref_rows(c, m, m if backward else m - 1, row & (2 * m - 1))
            e = jnp.exp2(-jnp.abs(c - g))
            qt, kt = (q * e).astype(BF16), (kk * e).astype(BF16)
        else:
            zeros = jnp.zeros((m, HEAD), F32)
            q_rows, k_rows = [], []
            for blk in range(CHUNK // (2 * m)):
                first = slice(2 * m * blk, 2 * m * blk + m)
                second = slice(2 * m * blk + m, 2 * m * (blk + 1))
                if backward:
                    g = c[2 * m * blk + m:2 * m * blk + m + 1, :]
                    q_rows += [q[first] * jnp.exp2(g - c[first]), zeros]
                    k_rows += [zeros, kk[second] * jnp.exp2(c[second] - g)]
                else:
                    g = c[2 * m * blk + m - 1:2 * m * blk + m, :]
                    q_rows += [zeros, q[second] * jnp.exp2(c[second] - g)]
                    k_rows += [kk[first] * jnp.exp2(g - c[first]), zeros]
            qt = jnp.concatenate(q_rows, axis=0).astype(BF16)
            kt = jnp.concatenate(k_rows, axis=0).astype(BF16)
        yield
        scores = jnp.where(xor_rc < m, scores, _dot_nt(qt, kt))
        yield
    col = row ^ xor_rc
    scores = jnp.where((row <= col) if backward else (row >= col), scores, 0.0).astype(BF16)

    if backward:
        q_dec = q * jnp.exp2(total - c)
        k_dec = kk * jnp.exp2(c)
    else:
        q_dec = q * jnp.exp2(c)
        k_dec = kk * jnp.exp2(total - c)
    q_dec, k_dec = q_dec.astype(BF16), k_dec.astype(BF16)
    yield
    o = _dot(scores, v_bf) + _dot_nt(q_dec, s_t.astype(BF16))
    s_new = s_t * jnp.exp2(total) + _dot_tn(v_bf, k_dec)
    return o, s_new


def _interleave(gens):
    results = [None] * len(gens)
    live = list(range(len(gens)))
    while live:
        still = []
        for i in live:
            try:
                next(gens[i])
                still.append(i)
            except StopIteration as stop:
                results[i] = stop.value
        live = still
    return results


def _hgrn_kernel(*refs, n_seq, has_state_in, n_aliased):
    lo_ref, hi_ref, lb_ref, gn_ref = refs[:4]
    k = 4
    s0_ref = None
    if has_state_in:
        s0_ref = refs[k]
        k += 1
    k += n_aliased
    y_ref = refs[k]
    k += 1
    sout_ref = None
    if not has_state_in:
        sout_ref = refs[k]
        k += 1
    q_scr, of_scr, ob_scr, s_scr = refs[k:k + 4]

    def section(idx, h):
        off = idx * D_C + h * HEAD
        return (lo_ref, off) if off < HALF_C else (hi_ref, off - HALF_C)

    n_chunks = TILE // CHUNK
    cps = n_chunks // n_seq
    q_scr[...] = _silu_tanh(lo_ref[:, 0:D_C]) * (HEAD ** -0.5)

    r_io = lax.broadcasted_iota(jnp.int32, (CHUNK, CHUNK), 0)
    c_io = lax.broadcasted_iota(jnp.int32, (CHUNK, CHUNK), 1)
    tri = jnp.where(c_io <= r_io, 1.0, 0.0).astype(BF16)
    consts = (jnp.concatenate([tri, tri], axis=1), r_io, r_io ^ c_io)

    for h in range(N_HEADS_C):
        for d in range(2):
            if has_state_in:
                s_scr[2 * h + d] = s0_ref[0, 0, d, h].T
            else:
                s_scr[2 * h + d] = jnp.zeros((HEAD, HEAD), F32)

    def body(it, carry):
        s = it // cps
        j = it % cps
        chains = [(h, d) for h in range(N_HEADS_C) for d in range(2)]
        for g0 in range(0, len(chains), CHAINS):
            group = chains[g0:g0 + CHAINS]
            gens, rows_of = [], []
            for h, d in group:
                backward = d == 1
                ch = s * cps + ((cps - 1 - j) if backward else j)
                rows = pl.ds(pl.multiple_of(ch * CHUNK, CHUNK), CHUNK)
                i_ref, i_off = section(1, h)
                f_ref, f_off = section(2 + d, h)
                s_t = s_scr[2 * h + d]
                if not has_state_in:
                    s_t = jnp.where(j == 0, 0.0, s_t)
                gens.append(_hgrn_chunk(q_scr[rows, h * HEAD:(h + 1) * HEAD],
                                        i_ref[rows, i_off:i_off + HEAD].astype(BF16),
                                        f_ref[rows, f_off:f_off + HEAD],
                                        lb_ref[d, 0, :, h * HEAD:(h + 1) * HEAD], s_t, consts, backward))
                rows_of.append(rows)
            for (h, d), rows, (o, s_new) in zip(group, rows_of, _interleave(gens)):
                (ob_scr if d == 1 else of_scr)[rows, h * HEAD:(h + 1) * HEAD] = o
                s_scr[2 * h + d] = s_new
                if sout_ref is not None:
                    sout_ref[s, 0, d, h] = s_new.T
        return carry

    lax.fori_loop(0, n_chunks, body, 0)

    for h in range(N_HEADS_C):
        cols = slice(h * HEAD, (h + 1) * HEAD)
        g_ref, g_off = section(4, h)
        o = of_scr[:, cols] + ob_scr[:, cols]
        y = _rms(o) * gn_ref[0, :, cols] * _silu_tanh(g_ref[:, g_off:g_off + HEAD])
        y_ref[:, cols] = y.astype(y_ref.dtype)


def _hgrn_mix(proj, lower_bound, gn_g, state_in, layer, n_seq, tile0, prev_out, prev_states=None):
    has_in = state_in is not None
    blk0 = (3 * D_A + 2 * D_B) // HALF_C
    in_specs = [
        pl.BlockSpec((TILE, HALF_C), lambda i: (i + tile0, blk0)),
        pl.BlockSpec((TILE, HALF_C), lambda i: (i + tile0, blk0 + 1)),
        pl.BlockSpec((2, 1, 1, D_C), lambda i: (0, layer, 0, 0)),
        _layer_vec_spec(layer, D_C),
    ]
    args = [proj, proj, lower_bound, gn_g]
    if has_in:
        in_specs.append(pl.BlockSpec((1, 1, 2, N_HEADS_C, HEAD, HEAD), lambda i: (i, layer, 0, 0, 0, 0)))
        args.append(state_in)
    aliases = {}
    for out_idx, prev in ((0, prev_out), (1, prev_states)):
        if prev is not None:
            in_specs.append(pl.BlockSpec(memory_space=pl.ANY))
            args.append(prev)
            aliases[len(args) - 1] = out_idx
    out_specs = [pl.BlockSpec((TILE, D_C), lambda i: (i + tile0, 0))]
    out_shape = [jax.ShapeDtypeStruct((ROWS, D_C), BF16)]
    if not has_in:
        out_specs.append(pl.BlockSpec((n_seq, 1, 2, N_HEADS_C, HEAD, HEAD), lambda i: (i, layer, 0, 0, 0, 0)))
        out_shape.append(jax.ShapeDtypeStruct((STREAM_TILES * n_seq, DEPTH, 2, N_HEADS_C, HEAD, HEAD), F32))
    res = pl.pallas_call(
        functools.partial(_hgrn_kernel, n_seq=n_seq, has_state_in=has_in, n_aliased=len(aliases)),
        grid=(STREAM_TILES,),
        in_specs=in_specs,
        out_specs=out_specs,
        out_shape=out_shape,
        scratch_shapes=[pltpu.VMEM((TILE, D_C), F32), pltpu.VMEM((TILE, D_C), F32),
                        pltpu.VMEM((TILE, D_C), F32), pltpu.VMEM((2 * N_HEADS_C, HEAD, HEAD), F32)],
        input_output_aliases=aliases,
        compiler_params=_params("arbitrary"),
        name="hgrn_mix",
    )(*args)
    return (res[0], None) if has_in else (res[0], res[1])


ROUTER_LANES = 128
SUB = 128
N_SLAB = TILE // 128
OUT_ROWS = 256


def _route(lg, bias_ref):
    shape = lg[0].shape
    mx = functools.reduce(jnp.maximum, lg)
    ex = [jnp.exp(l - mx) for l in lg]
    den = functools.reduce(jnp.add, ex)
    sc = [e / den for e in ex]
    biased = [sc[e] + bias_ref[e:e + 1, :] for e in range(N_EXPERTS)]

    best = jnp.zeros(shape, jnp.int32)
    best_val = None
    for g in range(N_GROUPS):
        mem = biased[g * EXPERTS_PER_GROUP:(g + 1) * EXPERTS_PER_GROUP]
        pair = None
        for a in range(EXPERTS_PER_GROUP):
            for b in range(a + 1, EXPERTS_PER_GROUP):
                sm = mem[a] + mem[b]
                pair = sm if pair is None else jnp.maximum(pair, sm)
        if best_val is None:
            best_val = pair
        else:
            upd = pair > best_val
            best = jnp.where(upd, g, best)
            best_val = jnp.where(upd, pair, best_val)

    masked = [jnp.where(best == (e // EXPERTS_PER_GROUP), biased[e], MASK_VALUE) for e in range(N_EXPERTS)]
    v1, i1 = masked[0], jnp.zeros(shape, jnp.int32)
    for e in range(1, N_EXPERTS):
        upd = masked[e] > v1
        i1 = jnp.where(upd, e, i1)
        v1 = jnp.where(upd, masked[e], v1)
    v2 = jnp.full(shape, -jnp.inf, F32)
    i2 = jnp.full(shape, -1, jnp.int32)
    for e in range(N_EXPERTS):
        cand = jnp.where(i1 == e, -jnp.inf, masked[e])
        upd = cand > v2
        i2 = jnp.where(upd, e, i2)
        v2 = jnp.where(upd, cand, v2)
    w1 = functools.reduce(jnp.add, [jnp.where(i1 == e, sc[e], 0.0) for e in range(N_EXPERTS)])
    w2 = functools.reduce(jnp.add, [jnp.where(i2 == e, sc[e], 0.0) for e in range(N_EXPERTS)])
    inv = 1.0 / (w1 + w2)
    gates = [(jnp.where(i1 == e, w1, 0.0) + jnp.where(i2 == e, w2, 0.0)) * inv for e in range(N_EXPERTS)]
    return gates, best


def _plan(best):
    r_io = lax.broadcasted_iota(jnp.int32, (128, 128), 0)
    c_io = lax.broadcasted_iota(jnp.int32, (128, 128), 1)
    before = jnp.where(r_io < c_io, 1.0, 0.0).astype(BF16)
    ranks, counts = [], []
    for g in range(N_GROUPS):
        member = jnp.where(best == g, 1.0, 0.0)
        within = _dot(member.astype(BF16), before)
        tot = jnp.sum(member, axis=1, keepdims=True)
        run = jnp.zeros((1, 1), F32)
        offs = []
        for c in range(N_SLAB):
            offs.append(run)
            run = run + tot[c:c + 1]
        ranks.append((member, within + jnp.concatenate(offs, axis=0)))
        counts.append(run)
    subs = [jnp.floor((n + (SUB - 1.0)) * (1.0 / SUB)) for n in counts]
    start = jnp.zeros((1, 1), F32)
    starts = []
    pos = jnp.zeros(best.shape, F32)
    for g in range(N_GROUPS):
        starts.append(start)
        member, rank = ranks[g]
        pos = pos + member * (start * SUB + rank)
        start = start + subs[g]
    return pos.astype(jnp.int32), jnp.concatenate(starts + subs, axis=0)


def _outproj_chunk(rows, x_refs, yab_ref, yc_ref, w_ref, xo_ref, h_ref, mod, g_norm, wr_split):
    g1, sh2, sc2 = mod
    y = _dot(yab_ref[rows, :], w_ref[0, 0:D_A + D_B, :]) + _dot(yc_ref[rows, :], w_ref[0, D_A + D_B:, :])
    yield
    x = _stream_rows(x_refs, pl.program_id(0), rows) + g1 * y
    xo_ref[rows, :] = x
    h = _rms(x) * g_norm * (1.0 + sc2) + sh2
    h_hi = h.astype(BF16)
    h_ref[rows, :] = h_hi
    h_lo = (h - h_hi.astype(F32)).astype(BF16)
    yield
    prod = _dot(jnp.concatenate([h_hi, h_lo], axis=0), wr_split)
    logits = (prod[0:OUT_ROWS, 0:ROUTER_LANES] + prod[0:OUT_ROWS, ROUTER_LANES:]
              + prod[OUT_ROWS:, 0:ROUTER_LANES])
    yield
    return logits.T


def _outproj_kernel(*refs):
    (yab_ref, yc_ref, mod_ref, g_ref, w_ref, wr_ref, br_ref, xo_ref, h_ref, gates_ref, pos_ref,
     tab_ref) = refs[-12:]
    x_refs = refs[:-12]
    m = mod_ref[0, 0]
    mod = (m[:, 2 * D_MODEL:3 * D_MODEL], m[:, 3 * D_MODEL:4 * D_MODEL], m[:, 4 * D_MODEL:5 * D_MODEL])
    wr = wr_ref[...]
    wr_hi = wr.astype(BF16)
    wr_lo = (wr - wr_hi.astype(F32)).astype(BF16)
    wr_split = jnp.concatenate([wr_hi, wr_lo], axis=1)
    chunks = [_outproj_chunk(pl.ds(c * OUT_ROWS, OUT_ROWS), x_refs, yab_ref, yc_ref, w_ref, xo_ref, h_ref, mod,
                             g_ref[0], wr_split) for c in range(TILE // OUT_ROWS)]
    logits_t = _interleave(chunks)
    lg = [jnp.concatenate([lt[e:e + 1, k * 128:(k + 1) * 128] for lt in logits_t for k in range(OUT_ROWS // 128)],
                          axis=0) for e in range(N_EXPERTS)]
    gates, best = _route(lg, br_ref)
    for e in range(N_EXPERTS):
        gates_ref[0, e] = gates[e]
    pos, tab = _plan(best)
    pos_ref[0] = pos
    tab_ref[0] = jnp.broadcast_to(tab, (8, 128)).astype(jnp.int32)


def _outproj(x, y_ab, y_c, mod, norm_g, w_out_bf, w_router_pad, b_router, layer):
    n_blk = ROWS // TILE
    x_specs, x_args = _stream_specs(x, lambda i: i)
    return pl.pallas_call(
        _outproj_kernel,
        grid=(n_blk,),
        in_specs=x_specs + [
            pl.BlockSpec((TILE, D_A + D_B), lambda i: (i, 0)),
            pl.BlockSpec((TILE, D_C), lambda i: (i, 0)),
            _mod_spec(layer, lambda i: i),
            _layer_vec_spec(layer, D_MODEL),
            pl.BlockSpec((1, D_MODEL, D_MODEL), lambda i: (layer, 0, 0)),
            pl.BlockSpec((D_MODEL, ROUTER_LANES), lambda i: (0, 0)),
            pl.BlockSpec((N_EXPERTS, 1), lambda i: (0, 0)),
        ],
        out_specs=[
            pl.BlockSpec((TILE, D_MODEL), lambda i: (i, 0)),
            pl.BlockSpec((TILE, D_MODEL), lambda i: (i, 0)),
            pl.BlockSpec((1, N_EXPERTS, N_SLAB, 128), lambda i: (i, 0, 0, 0)),
            pl.BlockSpec((1, N_SLAB, 128), lambda i: (i, 0, 0)),
            pl.BlockSpec((1, 8, 128), lambda i: (i, 0, 0)),
        ],
        out_shape=[
            jax.ShapeDtypeStruct((ROWS, D_MODEL), F32),
            jax.ShapeDtypeStruct((ROWS, D_MODEL), BF16),
            jax.ShapeDtypeStruct((n_blk, N_EXPERTS, N_SLAB, 128), F32),
            jax.ShapeDtypeStruct((n_blk, N_SLAB, 128), jnp.int32),
            jax.ShapeDtypeStruct((n_blk, 8, 128), jnp.int32),
        ],
        compiler_params=_params("arbitrary"),
        name="outproj_router",
    )(*x_args, y_ab, y_c, mod, norm_g, w_out_bf, w_router_pad, b_router)


MOE_VMEM_LIMIT = 58 * 1024 * 1024


def _moe_kernel(tab_ref, x_ref, h_ref, pos_ref, gates_ref, mod_ref, wg_ref, wu_ref, wd_ref, fg_ref, *rest, final):
    blk = pl.program_id(0)
    grp = pl.program_id(1)
    if final:
        (out_ref,) = rest
    else:
        ng_ref, nu_ref, nd_ref, out_ref, ng_out, nu_out, nd_out = rest
        ng_out[0] = ng_ref[0, 0].astype(BF16)
        nu_out[0] = nu_ref[0, 0].astype(BF16)
        nd_out[0] = nd_ref[0, 0].astype(BF16)

    @pl.when(grp == 0)
    def _():
        out_ref[...] = jnp.zeros_like(out_ref)

    first = tab_ref[blk, grp]
    count = tab_ref[blk, N_GROUPS + grp]
    gates = gates_ref[0, 0]
    g_hi = gates.astype(BF16)
    g_lo = (gates - g_hi.astype(F32)).astype(BF16)
    g_split = jnp.concatenate([g_hi, g_lo], axis=0)

    def sorted_rows(s, n_sub):
        rows = n_sub * SUB
        r_io = lax.broadcasted_iota(jnp.int32, (rows, TILE), 0)
        sel = jnp.where(r_io == pos_ref[...] - (first + s) * SUB, 1.0, 0.0).astype(BF16)
        xs = _dot(sel, h_ref[...]).astype(BF16)
        gs = _dot_nt(sel, g_split)
        y = jnp.zeros((rows, D_MODEL), F32)
        for e in range(EXPERTS_PER_GROUP):
            gate = gs[:, e:e + 1] + gs[:, EXPERTS_PER_GROUP + e:EXPERTS_PER_GROUP + e + 1]
            act = _silu(_dot(xs, wg_ref[e])) * _dot(xs, wu_ref[e]) * gate
            y = y + _dot(act.astype(BF16), wd_ref[e])
        out_ref[...] += _dot_tn(sel, y.astype(BF16))

    def pair(p, carry):
        sorted_rows(2 * p, 2)
        return carry

    lax.fori_loop(0, count // 2, pair, 0)

    @pl.when(count % 2 == 1)
    def _():
        sorted_rows(count - 1, 1)

    @pl.when(grp == N_GROUPS - 1)
    def _():
        g2 = mod_ref[0, 0][:, 5 * D_MODEL:6 * D_MODEL]
        x = x_ref[...] + g2 * out_ref[...]
        if final:
            x = _rms(x) * fg_ref[...]
        out_ref[...] = x


def _moe(x, h, pos, tab, gates, mod, w_bf, w_next, final_g, layer):
    final = w_next is None
    n_blk = ROWS // TILE
    steps = n_blk * N_GROUPS
    halves = steps // N_EXPERTS
    in_specs = [
        pl.BlockSpec((TILE, D_MODEL), lambda i, g, t: (i, 0)),
        pl.BlockSpec((TILE, D_MODEL), lambda i, g, t: (i, 0)),
        pl.BlockSpec((1, TILE), lambda i, g, t: (0, i)),
        pl.BlockSpec((1, 1, EXPERTS_PER_GROUP, TILE), lambda i, g, t: (i, g, 0, 0)),
        pl.BlockSpec((1, 1, 1, N_MOD * D_MODEL), lambda i, g, t: (layer, _cond_of_tile(i), 0, 0)),
        pl.BlockSpec((EXPERTS_PER_GROUP, D_MODEL, D_EXPERT), lambda i, g, t: (g, 0, 0)),
        pl.BlockSpec((EXPERTS_PER_GROUP, D_MODEL, D_EXPERT), lambda i, g, t: (g, 0, 0)),
        pl.BlockSpec((EXPERTS_PER_GROUP, D_EXPERT, D_MODEL), lambda i, g, t: (g, 0, 0)),
        pl.BlockSpec((1, D_MODEL), lambda i, g, t: (0, 0)),
    ]
    out_specs = [pl.BlockSpec((TILE, D_MODEL), lambda i, g, t: (i, 0))]
    out_shape = [jax.ShapeDtypeStruct((ROWS, D_MODEL), F32)]
    args = [tab, x, h, pos, gates, mod, *w_bf, final_g]
    if not final:
        def slab(i, g):
            step = i * N_GROUPS + g
            return step // halves, step % halves

        for w in w_next:
            rows = w.shape[2] // halves
            in_specs.append(pl.BlockSpec((1, 1, rows, w.shape[3]),
                                         lambda i, g, t: (layer + 1, *slab(i, g), 0)))
            out_specs.append(pl.BlockSpec((1, rows, w.shape[3]), lambda i, g, t: (*slab(i, g), 0)))
            out_shape.append(jax.ShapeDtypeStruct(w.shape[1:], BF16))
            args.append(w)
    res = pl.pallas_call(
        functools.partial(_moe_kernel, final=final),
        grid_spec=pltpu.PrefetchScalarGridSpec(num_scalar_prefetch=1, grid=(n_blk, N_GROUPS),
                                               in_specs=in_specs, out_specs=out_specs),
        out_shape=out_shape,
        compiler_params=pltpu.CompilerParams(dimension_semantics=("arbitrary", "arbitrary"),
                                             vmem_limit_bytes=MOE_VMEM_LIMIT),
        name="moe",
    )(*args)
    return res[0], (None if final else tuple(res[1:]))


def kernel(x_prompt, x_sample, state_hgrn, c, c_ctx, norm_mix_g, norm_ffn_g, final_norm_g, w_ada, b_ada, w_in, w_out, conv_a_w, conv_b_w, conv_b_b, ln_b_g, ln_b_b, lb_logits, gnorm_c_g, w_router, b_router, w_gate, w_up, w_down):
    batch, seq, _ = x_prompt.shape
    dec_batch, dec_seq, _ = x_sample.shape
    assert batch * seq == STREAM_ROWS and dec_batch * dec_seq == STREAM_ROWS
    assert dec_seq == TILE and TILE % seq == 0 and dec_batch + 1 <= 8

    vec = lambda a: a.reshape(DEPTH, 1, a.shape[-1])
    w_out_bf = w_out.astype(BF16)
    w_experts = (w_gate, w_up, w_down)
    w_bf = tuple(w[0].astype(BF16) for w in w_experts)
    w_router_pad = jnp.pad(w_router, ((0, 0), (0, ROUTER_LANES - N_EXPERTS)))
    b_router2 = b_router.reshape(N_EXPERTS, 1)
    final_g = final_norm_g.reshape(1, D_MODEL)
    norm_mix, norm_ffn, gn_g = vec(norm_mix_g), vec(norm_ffn_g), vec(gnorm_c_g)
    cb_b, ln_g, ln_b = vec(conv_b_b), vec(ln_b_g), vec(ln_b_b)

    lower_bound = _lower_bound(lb_logits).reshape(2, DEPTH, 1, D_C)
    cond8 = jnp.concatenate([c_ctx[None, :], c, jnp.zeros((8 - 1 - dec_batch, D_MODEL), F32)], axis=0)
    mod = _adaln(cond8, w_ada, b_ada).reshape(DEPTH, 8, 1, N_MOD * D_MODEL)

    x = (x_prompt.reshape(STREAM_ROWS, D_MODEL), x_sample.reshape(STREAM_ROWS, D_MODEL))
    states = None
    for l in range(DEPTH):
        proj = _inproj(x, mod, norm_mix, w_in, l)
        y_ab = _conv_mix(proj, conv_a_w, conv_b_w, cb_b, ln_g, ln_b, l, seq, 0, None)
        y_ab = _conv_mix(proj, conv_a_w, conv_b_w, cb_b, ln_g, ln_b, l, GRID_W, STREAM_TILES, y_ab)
        y_c, states = _hgrn_mix(proj, lower_bound, gn_g, None, l, TILE // seq, 0, None, states)
        y_c, _ = _hgrn_mix(proj, lower_bound, gn_g, state_hgrn, l, 1, STREAM_TILES, y_c)
        x, h, gates, pos, tab = _outproj(x, y_ab, y_c, mod, norm_ffn, w_out_bf, w_router_pad, b_router2, l)
        x, w_bf = _moe(x, h, pos.reshape(1, ROWS), tab[:, :, 0],
                       gates.reshape(ROWS // TILE, N_GROUPS, EXPERTS_PER_GROUP, TILE), mod,
                       w_bf, w_experts if l + 1 < DEPTH else None, final_g, l)
    return (x[:STREAM_ROWS].reshape(batch, seq, D_MODEL), x[STREAM_ROWS:].reshape(dec_batch, dec_seq, D_MODEL),
            states)
```

```python
import functools

import jax
import jax.numpy as jnp
from jax import lax
from jax.experimental import pallas as pl
from jax.experimental.pallas import tpu as pltpu

F32 = jnp.float32
BF16 = jnp.bfloat16

D_MODEL = 1024
DEPTH = 4
GRID_W = 64
D_A = 256
D_B = 256
D_C = 512
N_HEADS_C = 4
HEAD = 128
CONV_A = 3
CONV_B = 31
N_EXPERTS = 16
N_GROUPS = 4
EXPERTS_PER_GROUP = 4
D_EXPERT = 512
N_MOD = 6
D_IN = 3840
EPS = 1e-6
MASK_VALUE = -1e9
LOG2E = 1.4426950408889634

STREAM_ROWS = 4096
ROWS = 2 * STREAM_ROWS
TILE = 1024
STREAM_TILES = STREAM_ROWS // TILE
CHUNK = 128
N_LEVELS = 7
VMEM_LIMIT = 52 * 1024 * 1024


def _params(*sem):
    return pltpu.CompilerParams(dimension_semantics=sem, vmem_limit_bytes=VMEM_LIMIT)


def _dot(a, b):
    return jnp.dot(a, b, preferred_element_type=F32)


def _dot_nt(a, b):
    return lax.dot_general(a, b, (((1,), (1,)), ((), ())), preferred_element_type=F32)


def _dot_tn(a, b):
    return lax.dot_general(a, b, (((0,), (0,)), ((), ())), preferred_element_type=F32)


def _sigmoid(x):
    return 1.0 / (1.0 + jnp.exp2(x * (-LOG2E)))


def _silu(x):
    return x * _sigmoid(x)


def _silu_tanh(x):
    return x * (0.5 * jnp.tanh(0.5 * x) + 0.5)


def _rms(x):
    return x * lax.rsqrt(jnp.mean(x * x, axis=-1, keepdims=True) + EPS)


def _cond_of_tile(i):
    return jnp.maximum(i - (STREAM_TILES - 1), 0)


def _lower_bound_kernel(lg_ref, out_ref):
    for d in range(2):
        x = lg_ref[d]
        mx = jnp.max(x, axis=0, keepdims=True)
        e = jnp.exp(x - mx)
        p = e / jnp.sum(e, axis=0, keepdims=True)
        run = jnp.zeros_like(p[0:1])
        rows = []
        for l in range(DEPTH):
            run = run + p[l:l + 1]
            rows.append(run - p[0:1])
        out_ref[d] = jnp.concatenate(rows, axis=0)


def _lower_bound(lb_logits):
    return pl.pallas_call(
        _lower_bound_kernel,
        out_shape=jax.ShapeDtypeStruct((2, DEPTH, D_C), F32),
        name="lower_bound",
    )(lb_logits)


def _adaln_kernel(cond_ref, w_ref, b_ref, out_ref):
    a = _silu(cond_ref[...]).astype(BF16)
    out_ref[0] = _dot(a, w_ref[0].astype(BF16)) + b_ref[0]


def _adaln(cond8, w_ada, b_ada):
    tn = 1536
    nt = (N_MOD * D_MODEL) // tn
    return pl.pallas_call(
        _adaln_kernel,
        grid=(DEPTH, nt),
        in_specs=[
            pl.BlockSpec((8, D_MODEL), lambda l, j: (0, 0)),
            pl.BlockSpec((1, D_MODEL, tn), lambda l, j: (l, 0, j)),
            pl.BlockSpec((1, 1, tn), lambda l, j: (l, 0, j)),
        ],
        out_specs=pl.BlockSpec((1, 8, tn), lambda l, j: (l, 0, j)),
        out_shape=jax.ShapeDtypeStruct((DEPTH, 8, N_MOD * D_MODEL), F32),
        compiler_params=_params("arbitrary", "arbitrary"),
        name="adaln",
    )(cond8, w_ada, b_ada.reshape(DEPTH, 1, N_MOD * D_MODEL))


def _mod_spec(layer, tile_of):
    return pl.BlockSpec((1, 1, 1, N_MOD * D_MODEL),
                        lambda *g: (layer, _cond_of_tile(tile_of(*g)), 0, 0))


def _layer_vec_spec(layer, width):
    return pl.BlockSpec((1, 1, width), lambda *g: (layer, 0, 0))


def _stream_rows(x_refs, tile, rows=slice(None)):
    if len(x_refs) == 1:
        return x_refs[0][rows, :]
    return jnp.where(tile < STREAM_TILES, x_refs[0][rows, :], x_refs[1][rows, :])


def _stream_specs(x, tile_of):
    if not isinstance(x, tuple):
        return [pl.BlockSpec((TILE, D_MODEL), lambda *g: (tile_of(*g), 0))], [x]
    return ([pl.BlockSpec((TILE, D_MODEL), lambda *g: (jnp.minimum(tile_of(*g), STREAM_TILES - 1), 0)),
             pl.BlockSpec((TILE, D_MODEL), lambda *g: (jnp.maximum(tile_of(*g) - STREAM_TILES, 0), 0))],
            list(x))


PAD_B = 16
CONV_ROWS = 64


PROJ_AB = 3 * D_A + 2 * D_B


def _conv_body(p_ref, wa_ref, wb_ref, bb_ref, lng_ref, lnb_ref, out_ref, u_scr, c_scr, sh_scr, seq_len):
    latent = seq_len == GRID_W
    a_b = p_ref[:, 0:D_A]
    z = p_ref[:, D_A:2 * D_A] * p_ref[:, 2 * D_A:3 * D_A]
    pos = lax.broadcasted_iota(jnp.int32, (TILE, D_A), 0) % seq_len
    z_prev = jnp.where(pos == 0, 0.0, pltpu.roll(z, 1, axis=0))
    z_next = jnp.where(pos == seq_len - 1, 0.0, pltpu.roll(z, TILE - 1, axis=0))
    wa = wa_ref[0]
    y_a = a_b * (wa[0:1] * z_prev + wa[1:2] * z + wa[2:3] * z_next)
    out_ref[:, 0:D_A] = y_a.astype(out_ref.dtype)

    u = p_ref[:, 3 * D_A:3 * D_A + D_B] * _sigmoid(p_ref[:, 3 * D_A + D_B:3 * D_A + 2 * D_B])
    if latent:
        n_r = TILE // GRID_W
        u_scr[0:TILE, :] = u

        def row_body(r, carry):
            acc = jnp.zeros((GRID_W, D_B), F32)
            for rp in range(n_r):
                w = wb_ref[0, pl.ds(CONV_B // 2 + rp - r, 1), :]
                acc = acc + w * u_scr[rp * GRID_W:(rp + 1) * GRID_W, :]
            c_scr[pl.ds(pl.multiple_of(r * GRID_W, GRID_W), GRID_W), :] = acc
            return carry

        lax.fori_loop(0, n_r, row_body, 0)
    else:
        n_seq = TILE // seq_len
        padded = seq_len + 2 * PAD_B
        zeros = jnp.zeros((PAD_B, D_B), F32)
        for s in range(n_seq):
            u_scr[s * padded:s * padded + PAD_B, :] = zeros
            u_scr[s * padded + PAD_B:s * padded + PAD_B + seq_len, :] = u[s * seq_len:(s + 1) * seq_len]
            u_scr[s * padded + PAD_B + seq_len:(s + 1) * padded, :] = zeros
        u_scr[n_seq * padded:n_seq * padded + 8, :] = jnp.zeros((8, D_B), F32)
        for b in range(1, 8):
            sh_scr[b - 1] = u_scr[b:b + n_seq * padded, :]
        for s in range(n_seq):
            for c in range(seq_len // CONV_ROWS):
                base = s * padded + c * CONV_ROWS
                acc = jnp.zeros((CONV_ROWS, D_B), F32)
                for j in range(CONV_B):
                    off = j + PAD_B - CONV_B // 2
                    lo = base + off - off % 8
                    if off % 8 == 0:
                        tap = u_scr[lo:lo + CONV_ROWS, :]
                    else:
                        tap = sh_scr[off % 8 - 1, lo:lo + CONV_ROWS, :]
                    acc = acc + wb_ref[0, j:j + 1, :] * tap
                c_scr[s * seq_len + c * CONV_ROWS:s * seq_len + (c + 1) * CONV_ROWS, :] = acc

    v = c_scr[...] + bb_ref[0]
    mu = jnp.mean(v, axis=-1, keepdims=True)
    d = v - mu
    var = jnp.mean(d * d, axis=-1, keepdims=True)
    ln = d * lax.rsqrt(var + EPS) * lng_ref[0] + lnb_ref[0]
    out_ref[:, D_A:D_A + D_B] = _silu(ln).astype(out_ref.dtype)


def _conv_kernel(*refs, ctx_len):
    (mod_ref, g_ref, w_ref, wa_ref, wb_ref, bb_ref, lng_ref, lnb_ref, out_ref, h_ref,
     p_scr, u_scr, c_scr, sh_scr) = refs[-14:]
    x_refs = refs[:-14]
    i = pl.program_id(0)
    m = mod_ref[0, 0]
    h = _rms(_stream_rows(x_refs, i)) * g_ref[0] * (1.0 + m[:, D_MODEL:2 * D_MODEL]) + m[:, 0:D_MODEL]
    h = h.astype(BF16)
    h_ref[...] = h
    p_scr[...] = _dot(h, w_ref[0])
    mix = (p_scr, wa_ref, wb_ref, bb_ref, lng_ref, lnb_ref, out_ref, u_scr, c_scr, sh_scr)

    @pl.when(i < STREAM_TILES)
    def _():
        _conv_body(*mix, ctx_len)

    @pl.when(i >= STREAM_TILES)
    def _():
        _conv_body(*mix, GRID_W)


def _conv_mix(x, mod, norm_g, w_in_bf, conv_a_w, conv_b_w, conv_b_b, ln_g, ln_b, layer, ctx_len):
    pad_rows = (TILE // ctx_len) * (ctx_len + 2 * PAD_B)
    x_specs, x_args = _stream_specs(x, lambda i: i)
    return pl.pallas_call(
        functools.partial(_conv_kernel, ctx_len=ctx_len),
        grid=(ROWS // TILE,),
        in_specs=x_specs + [
            _mod_spec(layer, lambda i: i),
            _layer_vec_spec(layer, D_MODEL),
            pl.BlockSpec((1, D_MODEL, PROJ_AB), lambda i: (layer, 0, 0)),
            pl.BlockSpec((1, CONV_A, D_A), lambda i: (layer, 0, 0)),
            pl.BlockSpec((1, CONV_B, D_B), lambda i: (layer, 0, 0)),
            _layer_vec_spec(layer, D_B), _layer_vec_spec(layer, D_B), _layer_vec_spec(layer, D_B),
        ],
        out_specs=[pl.BlockSpec((TILE, D_A + D_B), lambda i: (i, 0)),
                   pl.BlockSpec((TILE, D_MODEL), lambda i: (i, 0))],
        out_shape=[jax.ShapeDtypeStruct((ROWS, D_A + D_B), BF16), jax.ShapeDtypeStruct((ROWS, D_MODEL), BF16)],
        scratch_shapes=[pltpu.VMEM((TILE, PROJ_AB), F32), pltpu.VMEM((pad_rows + 8, D_B), F32),
                        pltpu.VMEM((TILE, D_B), F32), pltpu.VMEM((7, pad_rows, D_B), F32)],
        compiler_params=_params("arbitrary"),
        name="conv_mix",
    )(*x_args, mod, norm_g, w_in_bf, conv_a_w, conv_b_w, conv_b_b, ln_g, ln_b)


def _ref_rows(c, m, ref_pos, pos_in_block):
    size = 2 * m
    if size >= 8:
        pieces = [jnp.broadcast_to(c[b * size + ref_pos:b * size + ref_pos + 1, :], (size, HEAD))
                  for b in range(CHUNK // size)]
        return pieces[0] if len(pieces) == 1 else jnp.concatenate(pieces, axis=0)
    if size == 4:
        c3 = c.reshape(CHUNK // 8, 8, HEAD)
        sub = lax.broadcasted_iota(jnp.int32, c3.shape, 1)
        lo = jnp.broadcast_to(c3[:, ref_pos:ref_pos + 1, :], c3.shape)
        hi = jnp.broadcast_to(c3[:, 4 + ref_pos:5 + ref_pos, :], c3.shape)
        return jnp.where(sub < 4, lo, hi).reshape(CHUNK, HEAD)
    out = c
    for p in range(size):
        shift = p - ref_pos
        if shift == 0:
            continue
        out = jnp.where(pos_in_block == p, pltpu.roll(c, shift % CHUNK, axis=0), out)
    return out


def _hgrn_chunk(q, v_bf, x, lb, s_t, consts, backward):
    tri2, row, xor_rc = consts
    sg = _sigmoid(x)
    one_m = 1.0 - lb
    lf = jnp.log2(lb + one_m * sg)
    kk = one_m * (1.0 - sg)
    lf_hi = lf.astype(BF16)
    lf_lo = (lf - lf_hi.astype(F32)).astype(BF16)
    yield
    cb = _dot(tri2, jnp.concatenate([lf_hi, lf_lo], axis=0))
    total = cb[CHUNK - 1:CHUNK, :]
    c = cb - lf if backward else cb
    scores = _dot_nt(q.astype(BF16), kk.astype(BF16))
    yield

    for b in range(N_LEVELS):
        m = 1 << b
        if m < 8:
            g = _ref_rows(c, m, m if backward else m - 1, row & (2 * m - 1))
            e = jnp.exp2(-jnp.abs(c - g))
            qt, kt = (q * e).astype(BF16), (kk * e).astype(BF16)
        else:
            zeros = jnp.zeros((m, HEAD), F32)
            q_rows, k_rows = [], []
            for blk in range(CHUNK // (2 * m)):
                first = slice(2 * m * blk, 2 * m * blk + m)
                second = slice(2 * m * blk + m, 2 * m * (blk + 1))
                if backward:
                    g = c[2 * m * blk + m:2 * m * blk + m + 1, :]
                    q_rows += [q[first] * jnp.exp2(g - c[first]), zeros]
                    k_rows += [zeros, kk[second] * jnp.exp2(c[second] - g)]
                else:
                    g = c[2 * m * blk + m - 1:2 * m * blk + m, :]
                    q_rows += [zeros, q[second] * jnp.exp2(c[second] - g)]
                    k_rows += [kk[first] * jnp.exp2(g - c[first]), zeros]
            qt = jnp.concatenate(q_rows, axis=0).astype(BF16)
            kt = jnp.concatenate(k_rows, axis=0).astype(BF16)
        yield
        scores = jnp.where(xor_rc < m, scores, _dot_nt(qt, kt))
        yield
    col = row ^ xor_rc
    scores = jnp.where((row <= col) if backward else (row >= col), scores, 0.0).astype(BF16)

    if backward:
        q_dec = q * jnp.exp2(total - c)
        k_dec = kk * jnp.exp2(c)
    else:
        q_dec = q * jnp.exp2(c)
        k_dec = kk * jnp.exp2(total - c)
    q_dec, k_dec = q_dec.astype(BF16), k_dec.astype(BF16)
    yield
    o = _dot(scores, v_bf) + _dot_nt(q_dec, s_t.astype(BF16))
    s_new = s_t * jnp.exp2(total) + _dot_tn(v_bf, k_dec)
    return o, s_new


def _interleave(gens):
    results = [None] * len(gens)
    live = list(range(len(gens)))
    while live:
        still = []
        for i in live:
            try:
                next(gens[i])
                still.append(i)
            except StopIteration as stop:
                results[i] = stop.value
        live = still
    return results


PROJ_C = 5 * D_C
PIECE = 256
N_PIECES = PROJ_C // PIECE
PIECES_EARLY = 3


def _hgrn_phase(cur, nxt, hn_ref, w_ref, lb_ref, gn_ref, s0_ref, y_ref, sout_ref,
                q_scr, of_scr, ob_scr, s_scr, n_seq):
    has_state_in = s0_ref is not None
    n_chunks = TILE // CHUNK
    cps = n_chunks // n_seq

    def head_cols(section, h, rows=slice(None)):
        c0 = section * D_C + h * HEAD
        return cur[c0 // PIECE, rows, c0 % PIECE:c0 % PIECE + HEAD]

    row_blocks = [slice(r0, r0 + OUT_ROWS) for r0 in range(0, TILE, OUT_ROWS)]

    def project_next(pieces):
        for p in pieces:
            for rows in row_blocks:
                nxt[p, rows, :] = _dot(hn_ref[rows, :], w_ref[0, p])
                yield

    def queries():
        for p in range(D_C // PIECE):
            for rows in row_blocks:
                q_scr[rows, p * PIECE:(p + 1) * PIECE] = _silu_tanh(cur[p, rows, :]) * (HEAD ** -0.5)
                yield

    _interleave([queries(), project_next(range(PIECES_EARLY))])

    r_io = lax.broadcasted_iota(jnp.int32, (CHUNK, CHUNK), 0)
    c_io = lax.broadcasted_iota(jnp.int32, (CHUNK, CHUNK), 1)
    tri = jnp.where(c_io <= r_io, 1.0, 0.0).astype(BF16)
    consts = (jnp.concatenate([tri, tri], axis=1), r_io, r_io ^ c_io)

    for h in range(N_HEADS_C):
        for d in range(2):
            if has_state_in:
                s_scr[2 * h + d] = s0_ref[0, 0, d, h].T
            else:
                s_scr[2 * h + d] = jnp.zeros((HEAD, HEAD), F32)

    def body(it, carry):
        s = it // cps
        j = it % cps
        gens, where = [], []
        for h in range(N_HEADS_C):
            for d in range(2):
                backward = d == 1
                ch = s * cps + ((cps - 1 - j) if backward else j)
                rows = pl.ds(pl.multiple_of(ch * CHUNK, CHUNK), CHUNK)
                s_t = s_scr[2 * h + d]
                if not has_state_in:
                    s_t = jnp.where(j == 0, 0.0, s_t)
                gens.append(_hgrn_chunk(q_scr[rows, h * HEAD:(h + 1) * HEAD], head_cols(1, h, rows).astype(BF16),
                                        head_cols(2 + d, h, rows), lb_ref[d, 0, :, h * HEAD:(h + 1) * HEAD],
                                        s_t, consts, backward))
                where.append((h, d, rows))
        results = _interleave(gens)
        for (h, d, rows), (o, s_new) in zip(where, results):
            (ob_scr if d == 1 else of_scr)[rows, h * HEAD:(h + 1) * HEAD] = o
            s_scr[2 * h + d] = s_new
            if not has_state_in:
                sout_ref[s, 0, d, h] = s_new.T
        return carry

    lax.fori_loop(0, n_chunks, body, 0)

    def outputs():
        for h in range(N_HEADS_C):
            cols = slice(h * HEAD, (h + 1) * HEAD)
            for rows in row_blocks:
                o = of_scr[rows, cols] + ob_scr[rows, cols]
                y = _rms(o) * gn_ref[0, :, cols] * _silu_tanh(head_cols(4, h, rows))
                y_ref[rows, cols] = y.astype(y_ref.dtype)
                yield

    _interleave([outputs(), project_next(range(PIECES_EARLY, N_PIECES))])


def _hgrn_kernel(hn_ref, w_ref, lb_ref, gn_ref, s0_ref, y_ref, sout_ref,
                 q_scr, of_scr, ob_scr, s_scr, proj_scr, *, ctx_seqs):
    i = pl.program_id(0)

    @pl.when(i == 0)
    def _():
        for p in range(N_PIECES):
            proj_scr[0, p] = _dot(hn_ref[...], w_ref[0, p])

    slot = (i + 1) % 2
    cur, nxt = proj_scr.at[slot], proj_scr.at[1 - slot]
    shared = (cur, nxt, hn_ref, w_ref, lb_ref, gn_ref)
    scratch = (q_scr, of_scr, ob_scr, s_scr)

    @pl.when((i >= 1) & (i <= STREAM_TILES))
    def _():
        _hgrn_phase(*shared, None, y_ref, sout_ref, *scratch, ctx_seqs)

    @pl.when(i > STREAM_TILES)
    def _():
        _hgrn_phase(*shared, s0_ref, y_ref, None, *scratch, 1)


def _hgrn_mix(h_bf, w_c, lower_bound, gn_g, state_in, prev_states, layer, ctx_seqs):
    n_tiles = ROWS // TILE
    tile = lambda i: jnp.maximum(i - 1, 0)
    in_specs = [
        pl.BlockSpec((TILE, D_MODEL), lambda i: (jnp.minimum(i, n_tiles - 1), 0)),
        pl.BlockSpec((1, N_PIECES, D_MODEL, PIECE), lambda i: (layer, 0, 0, 0)),
        pl.BlockSpec((2, 1, 1, D_C), lambda i: (0, layer, 0, 0)),
        _layer_vec_spec(layer, D_C),
        pl.BlockSpec((1, 1, 2, N_HEADS_C, HEAD, HEAD),
                     lambda i: (jnp.maximum(tile(i) - STREAM_TILES, 0), layer, 0, 0, 0, 0)),
    ]
    args = [h_bf, w_c, lower_bound, gn_g, state_in]
    aliases = {}
    if prev_states is not None:
        in_specs.append(pl.BlockSpec(memory_space=pl.ANY))
        args.append(prev_states)
        aliases[len(args) - 1] = 1
    kern = functools.partial(_hgrn_kernel, ctx_seqs=ctx_seqs)
    if prev_states is not None:
        kern = functools.partial(_drop_operand, kern, len(args) - 1)
    y, states = pl.pallas_call(
        kern,
        grid=(n_tiles + 1,),
        in_specs=in_specs,
        out_specs=[
            pl.BlockSpec((TILE, D_C), lambda i: (tile(i), 0)),
            pl.BlockSpec((ctx_seqs, 1, 2, N_HEADS_C, HEAD, HEAD),
                         lambda i: (jnp.minimum(tile(i), STREAM_TILES - 1), layer, 0, 0, 0, 0)),
        ],
        out_shape=[
            jax.ShapeDtypeStruct((ROWS, D_C), BF16),
            jax.ShapeDtypeStruct((STREAM_TILES * ctx_seqs, DEPTH, 2, N_HEADS_C, HEAD, HEAD), F32),
        ],
        scratch_shapes=[pltpu.VMEM((TILE, D_C), F32), pltpu.VMEM((TILE, D_C), F32),
                        pltpu.VMEM((TILE, D_C), F32), pltpu.VMEM((2 * N_HEADS_C, HEAD, HEAD), F32),
                        pltpu.VMEM((2, N_PIECES, TILE, PIECE), F32)],
        input_output_aliases=aliases,
        compiler_params=_params("arbitrary"),
        name="hgrn_mix",
    )(*args)
    return y, states


def _drop_operand(kern, index, *refs):
    return kern(*refs[:index], *refs[index + 1:])


ROUTER_LANES = 128
SUB = 128
N_SLAB = TILE // 128
OUT_ROWS = 256


def _route(lg, bias_ref):
    shape = lg[0].shape
    mx = functools.reduce(jnp.maximum, lg)
    ex = [jnp.exp(l - mx) for l in lg]
    den = functools.reduce(jnp.add, ex)
    sc = [e / den for e in ex]
    biased = [sc[e] + bias_ref[e:e + 1, :] for e in range(N_EXPERTS)]

    best = jnp.zeros(shape, jnp.int32)
    best_val = None
    for g in range(N_GROUPS):
        mem = biased[g * EXPERTS_PER_GROUP:(g + 1) * EXPERTS_PER_GROUP]
        pair = None
        for a in range(EXPERTS_PER_GROUP):
            for b in range(a + 1, EXPERTS_PER_GROUP):
                sm = mem[a] + mem[b]
                pair = sm if pair is None else jnp.maximum(pair, sm)
        if best_val is None:
            best_val = pair
        else:
            upd = pair > best_val
            best = jnp.where(upd, g, best)
            best_val = jnp.where(upd, pair, best_val)

    masked = [jnp.where(best == (e // EXPERTS_PER_GROUP), biased[e], MASK_VALUE) for e in range(N_EXPERTS)]
    v1, i1 = masked[0], jnp.zeros(shape, jnp.int32)
    for e in range(1, N_EXPERTS):
        upd = masked[e] > v1
        i1 = jnp.where(upd, e, i1)
        v1 = jnp.where(upd, masked[e], v1)
    v2 = jnp.full(shape, -jnp.inf, F32)
    i2 = jnp.full(shape, -1, jnp.int32)
    for e in range(N_EXPERTS):
        cand = jnp.where(i1 == e, -jnp.inf, masked[e])
        upd = cand > v2
        i2 = jnp.where(upd, e, i2)
        v2 = jnp.where(upd, cand, v2)
    w1 = functools.reduce(jnp.add, [jnp.where(i1 == e, sc[e], 0.0) for e in range(N_EXPERTS)])
    w2 = functools.reduce(jnp.add, [jnp.where(i2 == e, sc[e], 0.0) for e in range(N_EXPERTS)])
    inv = 1.0 / (w1 + w2)
    gates = [(jnp.where(i1 == e, w1, 0.0) + jnp.where(i2 == e, w2, 0.0)) * inv for e in range(N_EXPERTS)]
    return gates, best


def _plan(best):
    r_io = lax.broadcasted_iota(jnp.int32, (128, 128), 0)
    c_io = lax.broadcasted_iota(jnp.int32, (128, 128), 1)
    before = jnp.where(r_io < c_io, 1.0, 0.0).astype(BF16)
    ranks, counts = [], []
    for g in range(N_GROUPS):
        member = jnp.where(best == g, 1.0, 0.0)
        within = _dot(member.astype(BF16), before)
        tot = jnp.sum(member, axis=1, keepdims=True)
        run = jnp.zeros((1, 1), F32)
        offs = []
        for c in range(N_SLAB):
            offs.append(run)
            run = run + tot[c:c + 1]
        ranks.append((member, within + jnp.concatenate(offs, axis=0)))
        counts.append(run)
    subs = [jnp.floor((n + (SUB - 1.0)) * (1.0 / SUB)) for n in counts]
    start = jnp.zeros((1, 1), F32)
    starts = []
    pos = jnp.zeros(best.shape, F32)
    for g in range(N_GROUPS):
        starts.append(start)
        member, rank = ranks[g]
        pos = pos + member * (start * SUB + rank)
        start = start + subs[g]
    return pos.astype(jnp.int32), jnp.concatenate(starts + subs, axis=0)


def _outproj_chunk(rows, x_refs, yab_ref, yc_ref, w_ref, xo_ref, h_ref, mod, g_norm, wr_split):
    g1, sh2, sc2 = mod
    y = _dot(yab_ref[rows, :], w_ref[0, 0:D_A + D_B, :]) + _dot(yc_ref[rows, :], w_ref[0, D_A + D_B:, :])
    yield
    x = _stream_rows(x_refs, pl.program_id(0), rows) + g1 * y
    xo_ref[rows, :] = x
    h = _rms(x) * g_norm * (1.0 + sc2) + sh2
    h_hi = h.astype(BF16)
    h_ref[rows, :] = h_hi
    h_lo = (h - h_hi.astype(F32)).astype(BF16)
    yield
    prod = _dot(jnp.concatenate([h_hi, h_lo], axis=0), wr_split)
    logits = (prod[0:OUT_ROWS, 0:ROUTER_LANES] + prod[0:OUT_ROWS, ROUTER_LANES:]
              + prod[OUT_ROWS:, 0:ROUTER_LANES])
    yield
    return logits.T


def _outproj_kernel(*refs):
    (yab_ref, yc_ref, mod_ref, g_ref, w_ref, wr_ref, br_ref, xo_ref, h_ref, gates_ref, pos_ref,
     tab_ref) = refs[-12:]
    x_refs = refs[:-12]
    m = mod_ref[0, 0]
    mod = (m[:, 2 * D_MODEL:3 * D_MODEL], m[:, 3 * D_MODEL:4 * D_MODEL], m[:, 4 * D_MODEL:5 * D_MODEL])
    wr = wr_ref[...]
    wr_hi = wr.astype(BF16)
    wr_lo = (wr - wr_hi.astype(F32)).astype(BF16)
    wr_split = jnp.concatenate([wr_hi, wr_lo], axis=1)
    chunks = [_outproj_chunk(pl.ds(c * OUT_ROWS, OUT_ROWS), x_refs, yab_ref, yc_ref, w_ref, xo_ref, h_ref, mod,
                             g_ref[0], wr_split) for c in range(TILE // OUT_ROWS)]
    logits_t = _interleave(chunks)
    lg = [jnp.concatenate([lt[e:e + 1, k * 128:(k + 1) * 128] for lt in logits_t for k in range(OUT_ROWS // 128)],
                          axis=0) for e in range(N_EXPERTS)]
    gates, best = _route(lg, br_ref)
    for e in range(N_EXPERTS):
        gates_ref[0, e] = gates[e]
    pos, tab = _plan(best)
    pos_ref[0] = pos
    tab_ref[0] = jnp.broadcast_to(tab, (8, 128)).astype(jnp.int32)


def _outproj(x, y_ab, y_c, mod, norm_g, w_out_bf, w_router_pad, b_router, layer):
    n_blk = ROWS // TILE
    x_specs, x_args = _stream_specs(x, lambda i: i)
    return pl.pallas_call(
        _outproj_kernel,
        grid=(n_blk,),
        in_specs=x_specs + [
            pl.BlockSpec((TILE, D_A + D_B), lambda i: (i, 0)),
            pl.BlockSpec((TILE, D_C), lambda i: (i, 0)),
            _mod_spec(layer, lambda i: i),
            _layer_vec_spec(layer, D_MODEL),
            pl.BlockSpec((1, D_MODEL, D_MODEL), lambda i: (layer, 0, 0)),
            pl.BlockSpec((D_MODEL, ROUTER_LANES), lambda i: (0, 0)),
            pl.BlockSpec((N_EXPERTS, 1), lambda i: (0, 0)),
        ],
        out_specs=[
            pl.BlockSpec((TILE, D_MODEL), lambda i: (i, 0)),
            pl.BlockSpec((TILE, D_MODEL), lambda i: (i, 0)),
            pl.BlockSpec((1, N_EXPERTS, N_SLAB, 128), lambda i: (i, 0, 0, 0)),
            pl.BlockSpec((1, N_SLAB, 128), lambda i: (i, 0, 0)),
            pl.BlockSpec((1, 8, 128), lambda i: (i, 0, 0)),
        ],
        out_shape=[
            jax.ShapeDtypeStruct((ROWS, D_MODEL), F32),
            jax.ShapeDtypeStruct((ROWS, D_MODEL), BF16),
            jax.ShapeDtypeStruct((n_blk, N_EXPERTS, N_SLAB, 128), F32),
            jax.ShapeDtypeStruct((n_blk, N_SLAB, 128), jnp.int32),
            jax.ShapeDtypeStruct((n_blk, 8, 128), jnp.int32),
        ],
        compiler_params=_params("arbitrary"),
        name="outproj_router",
    )(*x_args, y_ab, y_c, mod, norm_g, w_out_bf, w_router_pad, b_router)


MOE_VMEM_LIMIT = 58 * 1024 * 1024


def _moe_kernel(tab_ref, x_ref, h_ref, pos_ref, gates_ref, mod_ref, wg_ref, wu_ref, wd_ref, fg_ref, *rest, final):
    blk = pl.program_id(0)
    grp = pl.program_id(1)
    if final:
        (out_ref,) = rest
    else:
        ng_ref, nu_ref, nd_ref, out_ref, ng_out, nu_out, nd_out = rest
        ng_out[0] = ng_ref[0, 0].astype(BF16)
        nu_out[0] = nu_ref[0, 0].astype(BF16)
        nd_out[0] = nd_ref[0, 0].astype(BF16)

    @pl.when(grp == 0)
    def _():
        out_ref[...] = jnp.zeros_like(out_ref)

    first = tab_ref[blk, grp]
    count = tab_ref[blk, N_GROUPS + grp]
    gates = gates_ref[0, 0]
    g_hi = gates.astype(BF16)
    g_lo = (gates - g_hi.astype(F32)).astype(BF16)
    g_split = jnp.concatenate([g_hi, g_lo], axis=0)

    def sorted_rows(s, n_sub):
        rows = n_sub * SUB
        r_io = lax.broadcasted_iota(jnp.int32, (rows, TILE), 0)
        sel = jnp.where(r_io == pos_ref[...] - (first + s) * SUB, 1.0, 0.0).astype(BF16)
        xs = _dot(sel, h_ref[...]).astype(BF16)
        gs = _dot_nt(sel, g_split)
        y = jnp.zeros((rows, D_MODEL), F32)
        for e in range(EXPERTS_PER_GROUP):
            gate = gs[:, e:e + 1] + gs[:, EXPERTS_PER_GROUP + e:EXPERTS_PER_GROUP + e + 1]
            act = _silu(_dot(xs, wg_ref[e])) * _dot(xs, wu_ref[e]) * gate
            y = y + _dot(act.astype(BF16), wd_ref[e])
        out_ref[...] += _dot_tn(sel, y.astype(BF16))

    def pair(p, carry):
        sorted_rows(2 * p, 2)
        return carry

    lax.fori_loop(0, count // 2, pair, 0)

    @pl.when(count % 2 == 1)
    def _():
        sorted_rows(count - 1, 1)

    @pl.when(grp == N_GROUPS - 1)
    def _():
        g2 = mod_ref[0, 0][:, 5 * D_MODEL:6 * D_MODEL]
        x = x_ref[...] + g2 * out_ref[...]
        if final:
            x = _rms(x) * fg_ref[...]
        out_ref[...] = x


def _moe(x, h, pos, tab, gates, mod, w_bf, w_next, final_g, layer):
    final = w_next is None
    n_blk = ROWS // TILE
    steps = n_blk * N_GROUPS
    halves = steps // N_EXPERTS
    in_specs = [
        pl.BlockSpec((TILE, D_MODEL), lambda i, g, t: (i, 0)),
        pl.BlockSpec((TILE, D_MODEL), lambda i, g, t: (i, 0)),
        pl.BlockSpec((1, TILE), lambda i, g, t: (0, i)),
        pl.BlockSpec((1, 1, EXPERTS_PER_GROUP, TILE), lambda i, g, t: (i, g, 0, 0)),
        pl.BlockSpec((1, 1, 1, N_MOD * D_MODEL), lambda i, g, t: (layer, _cond_of_tile(i), 0, 0)),
        pl.BlockSpec((EXPERTS_PER_GROUP, D_MODEL, D_EXPERT), lambda i, g, t: (g, 0, 0)),
        pl.BlockSpec((EXPERTS_PER_GROUP, D_MODEL, D_EXPERT), lambda i, g, t: (g, 0, 0)),
        pl.BlockSpec((EXPERTS_PER_GROUP, D_EXPERT, D_MODEL), lambda i, g, t: (g, 0, 0)),
        pl.BlockSpec((1, D_MODEL), lambda i, g, t: (0, 0)),
    ]
    out_specs = [pl.BlockSpec((TILE, D_MODEL), lambda i, g, t: (i, 0))]
    out_shape = [jax.ShapeDtypeStruct((ROWS, D_MODEL), F32)]
    args = [tab, x, h, pos, gates, mod, *w_bf, final_g]
    if not final:
        def slab(i, g):
            step = i * N_GROUPS + g
            return step // halves, step % halves

        for w in w_next:
            rows = w.shape[2] // halves
            in_specs.append(pl.BlockSpec((1, 1, rows, w.shape[3]),
                                         lambda i, g, t: (layer + 1, *slab(i, g), 0)))
            out_specs.append(pl.BlockSpec((1, rows, w.shape[3]), lambda i, g, t: (*slab(i, g), 0)))
            out_shape.append(jax.ShapeDtypeStruct(w.shape[1:], BF16))
            args.append(w)
    res = pl.pallas_call(
        functools.partial(_moe_kernel, final=final),
        grid_spec=pltpu.PrefetchScalarGridSpec(num_scalar_prefetch=1, grid=(n_blk, N_GROUPS),
                                               in_specs=in_specs, out_specs=out_specs),
        out_shape=out_shape,
        compiler_params=pltpu.CompilerParams(dimension_semantics=("arbitrary", "arbitrary"),
                                             vmem_limit_bytes=MOE_VMEM_LIMIT),
        name="moe",
    )(*args)
    return res[0], (None if final else tuple(res[1:]))


def kernel(x_prompt, x_sample, state_hgrn, c, c_ctx, norm_mix_g, norm_ffn_g, final_norm_g, w_ada, b_ada, w_in, w_out, conv_a_w, conv_b_w, conv_b_b, ln_b_g, ln_b_b, lb_logits, gnorm_c_g, w_router, b_router, w_gate, w_up, w_down):
    batch, seq, _ = x_prompt.shape
    dec_batch, dec_seq, _ = x_sample.shape
    assert batch * seq == STREAM_ROWS and dec_batch * dec_seq == STREAM_ROWS
    assert dec_seq == TILE and TILE % seq == 0 and dec_batch + 1 <= 8

    vec = lambda a: a.reshape(DEPTH, 1, a.shape[-1])
    w_in_bf, w_out_bf = w_in.astype(BF16), w_out.astype(BF16)
    w_c = w_in_bf[:, :, PROJ_AB:].reshape(DEPTH, D_MODEL, N_PIECES, PIECE).transpose(0, 2, 1, 3)
    w_experts = (w_gate, w_up, w_down)
    w_bf = tuple(w[0].astype(BF16) for w in w_experts)
    w_router_pad = jnp.pad(w_router, ((0, 0), (0, ROUTER_LANES - N_EXPERTS)))
    b_router2 = b_router.reshape(N_EXPERTS, 1)
    final_g = final_norm_g.reshape(1, D_MODEL)
    norm_mix, norm_ffn, gn_g = vec(norm_mix_g), vec(norm_ffn_g), vec(gnorm_c_g)
    cb_b, ln_g, ln_b = vec(conv_b_b), vec(ln_b_g), vec(ln_b_b)

    lower_bound = _lower_bound(lb_logits).reshape(2, DEPTH, 1, D_C)
    cond8 = jnp.concatenate([c_ctx[None, :], c, jnp.zeros((8 - 1 - dec_batch, D_MODEL), F32)], axis=0)
    mod = _adaln(cond8, w_ada, b_ada).reshape(DEPTH, 8, 1, N_MOD * D_MODEL)

    x = (x_prompt.reshape(STREAM_ROWS, D_MODEL), x_sample.reshape(STREAM_ROWS, D_MODEL))
    states = None
    for l in range(DEPTH):
        y_ab, h_mix = _conv_mix(x, mod, norm_mix, w_in_bf, conv_a_w, conv_b_w, cb_b, ln_g, ln_b, l, seq)
        y_c, states = _hgrn_mix(h_mix, w_c, lower_bound, gn_g, state_hgrn, states, l, TILE // seq)
        x, h, gates, pos, tab = _outproj(x, y_ab, y_c, mod, norm_ffn, w_out_bf, w_router_pad, b_router2, l)
        x, w_bf = _moe(x, h, pos.reshape(1, ROWS), tab[:, :, 0],
                       gates.reshape(ROWS // TILE, N_GROUPS, EXPERTS_PER_GROUP, TILE), mod,
                       w_bf, w_experts if l + 1 < DEPTH else None, final_g, l)
    return (x[:STREAM_ROWS].reshape(batch, seq, D_MODEL), x[STREAM_ROWS:].reshape(dec_batch, dec_seq, D_MODEL),
            states)
```

```python
import functools

import jax
import jax.numpy as jnp
from jax import lax
from jax.experimental import pallas as pl
from jax.experimental.pallas import tpu as pltpu

F32 = jnp.float32
BF16 = jnp.bfloat16

D_MODEL = 1024
DEPTH = 4
GRID_W = 64
D_A = 256
D_B = 256
D_C = 512
N_HEADS_C = 4
HEAD = 128
CONV_A = 3
CONV_B = 31
N_EXPERTS = 16
N_GROUPS = 4
EXPERTS_PER_GROUP = 4
D_EXPERT = 512
N_MOD = 6
D_IN = 3840
EPS = 1e-6
MASK_VALUE = -1e9
LOG2E = 1.4426950408889634

STREAM_ROWS = 4096
ROWS = 2 * STREAM_ROWS
TILE = 1024
STREAM_TILES = STREAM_ROWS // TILE
CHUNK = 128
N_LEVELS = 7
VMEM_LIMIT = 52 * 1024 * 1024


def _params(*sem):
    return pltpu.CompilerParams(dimension_semantics=sem, vmem_limit_bytes=VMEM_LIMIT)


def _dot(a, b):
    return jnp.dot(a, b, preferred_element_type=F32)


def _dot_nt(a, b):
    return lax.dot_general(a, b, (((1,), (1,)), ((), ())), preferred_element_type=F32)


def _dot_tn(a, b):
    return lax.dot_general(a, b, (((0,), (0,)), ((), ())), preferred_element_type=F32)


def _sigmoid(x):
    return 1.0 / (1.0 + jnp.exp2(x * (-LOG2E)))


def _silu(x):
    return x * _sigmoid(x)


def _silu_tanh(x):
    return x * (0.5 * jnp.tanh(0.5 * x) + 0.5)


def _rms(x):
    return x * lax.rsqrt(jnp.mean(x * x, axis=-1, keepdims=True) + EPS)


def _cond_of_tile(i):
    return jnp.maximum(i - (STREAM_TILES - 1), 0)


def _lower_bound_kernel(lg_ref, out_ref):
    for d in range(2):
        x = lg_ref[d]
        mx = jnp.max(x, axis=0, keepdims=True)
        e = jnp.exp(x - mx)
        p = e / jnp.sum(e, axis=0, keepdims=True)
        run = jnp.zeros_like(p[0:1])
        rows = []
        for l in range(DEPTH):
            run = run + p[l:l + 1]
            rows.append(run - p[0:1])
        out_ref[d] = jnp.concatenate(rows, axis=0)


def _lower_bound(lb_logits):
    return pl.pallas_call(
        _lower_bound_kernel,
        out_shape=jax.ShapeDtypeStruct((2, DEPTH, D_C), F32),
        name="lower_bound",
    )(lb_logits)


def _adaln_kernel(cond_ref, w_ref, b_ref, out_ref):
    a = _silu(cond_ref[...]).astype(BF16)
    out_ref[0] = _dot(a, w_ref[0].astype(BF16)) + b_ref[0]


def _adaln(cond8, w_ada, b_ada):
    tn = 1536
    nt = (N_MOD * D_MODEL) // tn
    return pl.pallas_call(
        _adaln_kernel,
        grid=(DEPTH, nt),
        in_specs=[
            pl.BlockSpec((8, D_MODEL), lambda l, j: (0, 0)),
            pl.BlockSpec((1, D_MODEL, tn), lambda l, j: (l, 0, j)),
            pl.BlockSpec((1, 1, tn), lambda l, j: (l, 0, j)),
        ],
        out_specs=pl.BlockSpec((1, 8, tn), lambda l, j: (l, 0, j)),
        out_shape=jax.ShapeDtypeStruct((DEPTH, 8, N_MOD * D_MODEL), F32),
        compiler_params=_params("arbitrary", "arbitrary"),
        name="adaln",
    )(cond8, w_ada, b_ada.reshape(DEPTH, 1, N_MOD * D_MODEL))


def _mod_spec(layer, tile_of):
    return pl.BlockSpec((1, 1, 1, N_MOD * D_MODEL),
                        lambda *g: (layer, _cond_of_tile(tile_of(*g)), 0, 0))


def _layer_vec_spec(layer, width):
    return pl.BlockSpec((1, 1, width), lambda *g: (layer, 0, 0))


def _stream_rows(x_refs, tile, rows=slice(None)):
    if len(x_refs) == 1:
        return x_refs[0][rows, :]
    return jnp.where(tile < STREAM_TILES, x_refs[0][rows, :], x_refs[1][rows, :])


def _stream_specs(x, tile_of):
    if not isinstance(x, tuple):
        return [pl.BlockSpec((TILE, D_MODEL), lambda *g: (tile_of(*g), 0))], [x]
    return ([pl.BlockSpec((TILE, D_MODEL), lambda *g: (jnp.minimum(tile_of(*g), STREAM_TILES - 1), 0)),
             pl.BlockSpec((TILE, D_MODEL), lambda *g: (jnp.maximum(tile_of(*g) - STREAM_TILES, 0), 0))],
            list(x))


PAD_B = 16
CONV_ROWS = 64


PROJ_AB = 3 * D_A + 2 * D_B


def _conv_body(p_ref, wa_ref, wb_ref, bb_ref, lng_ref, lnb_ref, out_ref, u_scr, c_scr, sh_scr, seq_len):
    latent = seq_len == GRID_W
    a_b = p_ref[:, 0:D_A]
    z = p_ref[:, D_A:2 * D_A] * p_ref[:, 2 * D_A:3 * D_A]
    pos = lax.broadcasted_iota(jnp.int32, (TILE, D_A), 0) % seq_len
    z_prev = jnp.where(pos == 0, 0.0, pltpu.roll(z, 1, axis=0))
    z_next = jnp.where(pos == seq_len - 1, 0.0, pltpu.roll(z, TILE - 1, axis=0))
    wa = wa_ref[0]
    y_a = a_b * (wa[0:1] * z_prev + wa[1:2] * z + wa[2:3] * z_next)
    out_ref[:, 0:D_A] = y_a.astype(out_ref.dtype)

    u = p_ref[:, 3 * D_A:3 * D_A + D_B] * _sigmoid(p_ref[:, 3 * D_A + D_B:3 * D_A + 2 * D_B])
    if latent:
        n_r = TILE // GRID_W
        u_scr[0:TILE, :] = u

        def row_body(r, carry):
            acc = jnp.zeros((GRID_W, D_B), F32)
            for rp in range(n_r):
                w = wb_ref[0, pl.ds(CONV_B // 2 + rp - r, 1), :]
                acc = acc + w * u_scr[rp * GRID_W:(rp + 1) * GRID_W, :]
            c_scr[pl.ds(pl.multiple_of(r * GRID_W, GRID_W), GRID_W), :] = acc
            return carry

        lax.fori_loop(0, n_r, row_body, 0)
    else:
        n_seq = TILE // seq_len
        padded = seq_len + 2 * PAD_B
        zeros = jnp.zeros((PAD_B, D_B), F32)
        for s in range(n_seq):
            u_scr[s * padded:s * padded + PAD_B, :] = zeros
            u_scr[s * padded + PAD_B:s * padded + PAD_B + seq_len, :] = u[s * seq_len:(s + 1) * seq_len]
            u_scr[s * padded + PAD_B + seq_len:(s + 1) * padded, :] = zeros
        u_scr[n_seq * padded:n_seq * padded + 8, :] = jnp.zeros((8, D_B), F32)
        for b in range(1, 8):
            sh_scr[b - 1] = u_scr[b:b + n_seq * padded, :]
        for s in range(n_seq):
            for c in range(seq_len // CONV_ROWS):
                base = s * padded + c * CONV_ROWS
                acc = jnp.zeros((CONV_ROWS, D_B), F32)
                for j in range(CONV_B):
                    off = j + PAD_B - CONV_B // 2
                    lo = base + off - off % 8
                    if off % 8 == 0:
                        tap = u_scr[lo:lo + CONV_ROWS, :]
                    else:
                        tap = sh_scr[off % 8 - 1, lo:lo + CONV_ROWS, :]
                    acc = acc + wb_ref[0, j:j + 1, :] * tap
                c_scr[s * seq_len + c * CONV_ROWS:s * seq_len + (c + 1) * CONV_ROWS, :] = acc

    v = c_scr[...] + bb_ref[0]
    mu = jnp.mean(v, axis=-1, keepdims=True)
    d = v - mu
    var = jnp.mean(d * d, axis=-1, keepdims=True)
    ln = d * lax.rsqrt(var + EPS) * lng_ref[0] + lnb_ref[0]
    out_ref[:, D_A:D_A + D_B] = _silu(ln).astype(out_ref.dtype)


def _conv_kernel(*refs, ctx_len):
    (mod_ref, g_ref, w_ref, wa_ref, wb_ref, bb_ref, lng_ref, lnb_ref, out_ref, h_ref,
     p_scr, u_scr, c_scr, sh_scr) = refs[-14:]
    x_refs = refs[:-14]
    i = pl.program_id(0)
    m = mod_ref[0, 0]
    h = _rms(_stream_rows(x_refs, i)) * g_ref[0] * (1.0 + m[:, D_MODEL:2 * D_MODEL]) + m[:, 0:D_MODEL]
    h = h.astype(BF16)
    h_ref[...] = h
    p_scr[...] = _dot(h, w_ref[0])
    mix = (p_scr, wa_ref, wb_ref, bb_ref, lng_ref, lnb_ref, out_ref, u_scr, c_scr, sh_scr)

    @pl.when(i < STREAM_TILES)
    def _():
        _conv_body(*mix, ctx_len)

    @pl.when(i >= STREAM_TILES)
    def _():
        _conv_body(*mix, GRID_W)


def _conv_mix(x, mod, norm_g, w_in_bf, conv_a_w, conv_b_w, conv_b_b, ln_g, ln_b, layer, ctx_len):
    pad_rows = (TILE // ctx_len) * (ctx_len + 2 * PAD_B)
    x_specs, x_args = _stream_specs(x, lambda i: i)
    return pl.pallas_call(
        functools.partial(_conv_kernel, ctx_len=ctx_len),
        grid=(ROWS // TILE,),
        in_specs=x_specs + [
            _mod_spec(layer, lambda i: i),
            _layer_vec_spec(layer, D_MODEL),
            pl.BlockSpec((1, D_MODEL, PROJ_AB), lambda i: (layer, 0, 0)),
            pl.BlockSpec((1, CONV_A, D_A), lambda i: (layer, 0, 0)),
            pl.BlockSpec((1, CONV_B, D_B), lambda i: (layer, 0, 0)),
            _layer_vec_spec(layer, D_B), _layer_vec_spec(layer, D_B), _layer_vec_spec(layer, D_B),
        ],
        out_specs=[pl.BlockSpec((TILE, D_A + D_B), lambda i: (i, 0)),
                   pl.BlockSpec((TILE, D_MODEL), lambda i: (i, 0))],
        out_shape=[jax.ShapeDtypeStruct((ROWS, D_A + D_B), BF16), jax.ShapeDtypeStruct((ROWS, D_MODEL), BF16)],
        scratch_shapes=[pltpu.VMEM((TILE, PROJ_AB), F32), pltpu.VMEM((pad_rows + 8, D_B), F32),
                        pltpu.VMEM((TILE, D_B), F32), pltpu.VMEM((7, pad_rows, D_B), F32)],
        compiler_params=_params("arbitrary"),
        name="conv_mix",
    )(*x_args, mod, norm_g, w_in_bf, conv_a_w, conv_b_w, conv_b_b, ln_g, ln_b)


def _ref_rows(c, m, ref_pos, pos_in_block):
    size = 2 * m
    if size >= 8:
        pieces = [jnp.broadcast_to(c[b * size + ref_pos:b * size + ref_pos + 1, :], (size, HEAD))
                  for b in range(CHUNK // size)]
        return pieces[0] if len(pieces) == 1 else jnp.concatenate(pieces, axis=0)
    if size == 4:
        c3 = c.reshape(CHUNK // 8, 8, HEAD)
        sub = lax.broadcasted_iota(jnp.int32, c3.shape, 1)
        lo = jnp.broadcast_to(c3[:, ref_pos:ref_pos + 1, :], c3.shape)
        hi = jnp.broadcast_to(c3[:, 4 + ref_pos:5 + ref_pos, :], c3.shape)
        return jnp.where(sub < 4, lo, hi).reshape(CHUNK, HEAD)
    out = c
    for p in range(size):
        shift = p - ref_pos
        if shift == 0:
            continue
        out = jnp.where(pos_in_block == p, pltpu.roll(c, shift % CHUNK, axis=0), out)
    return out


def _hgrn_chunk(q, v_bf, x, lb, s_t, consts, backward):
    tri2, row, xor_rc = consts
    sg = _sigmoid(x)
    one_m = 1.0 - lb
    lf = jnp.log2(lb + one_m * sg)
    kk = one_m * (1.0 - sg)
    lf_hi = lf.astype(BF16)
    lf_lo = (lf - lf_hi.astype(F32)).astype(BF16)
    yield
    cb = _dot(tri2, jnp.concatenate([lf_hi, lf_lo], axis=0))
    total = cb[CHUNK - 1:CHUNK, :]
    c = cb - lf if backward else cb
    scores = _dot_nt(q.astype(BF16), kk.astype(BF16))
    yield

    for b in range(N_LEVELS):
        m = 1 << b
        if m < 8:
            g = _ref_rows(c, m, m if backward else m - 1, row & (2 * m - 1))
            e = jnp.exp2(-jnp.abs(c - g))
            qt, kt = (q * e).astype(BF16), (kk * e).astype(BF16)
        else:
            zeros = jnp.zeros((m, HEAD), F32)
            q_rows, k_rows = [], []
            for blk in range(CHUNK // (2 * m)):
                first = slice(2 * m * blk, 2 * m * blk + m)
                second = slice(2 * m * blk + m, 2 * m * (blk + 1))
                if backward:
                    g = c[2 * m * blk + m:2 * m * blk + m + 1, :]
                    q_rows += [q[first] * jnp.exp2(g - c[first]), zeros]
                    k_rows += [zeros, kk[second] * jnp.exp2(c[second] - g)]
                else:
                    g = c[2 * m * blk + m - 1:2 * m * blk + m, :]
                    q_rows += [zeros, q[second] * jnp.exp2(c[second] - g)]
                    k_rows += [kk[first] * jnp.exp2(g - c[first]), zeros]
            qt = jnp.concatenate(q_rows, axis=0).astype(BF16)
            kt = jnp.concatenate(k_rows, axis=0).astype(BF16)
        yield
        scores = jnp.where(xor_rc < m, scores, _dot_nt(qt, kt))
        yield
    col = row ^ xor_rc
    scores = jnp.where((row <= col) if backward else (row >= col), scores, 0.0).astype(BF16)

    if backward:
        q_dec = q * jnp.exp2(total - c)
        k_dec = kk * jnp.exp2(c)
    else:
        q_dec = q * jnp.exp2(c)
        k_dec = kk * jnp.exp2(total - c)
    q_dec, k_dec = q_dec.astype(BF16), k_dec.astype(BF16)
    yield
    o = _dot(scores, v_bf) + _dot_nt(q_dec, s_t.astype(BF16))
    s_new = s_t * jnp.exp2(total) + _dot_tn(v_bf, k_dec)
    return o, s_new


def _interleave(gens):
    results = [None] * len(gens)
    live = list(range(len(gens)))
    while live:
        still = []
        for i in live:
            try:
                next(gens[i])
                still.append(i)
            except StopIteration as stop:
                results[i] = stop.value
        live = still
    return results


PROJ_C = 5 * D_C
PIECE = 256
N_PIECES = PROJ_C // PIECE
PIECES_EARLY = 3


def _hgrn_phase(cur, nxt, hn_ref, w_refs, lb_ref, gn_ref, s0_ref, y_ref, sout_ref,
                q_scr, of_scr, ob_scr, s_scr, n_seq):
    has_state_in = s0_ref is not None
    n_chunks = TILE // CHUNK
    cps = n_chunks // n_seq

    def head_cols(section, h, rows=slice(None)):
        c0 = section * D_C + h * HEAD
        return cur[c0 // PIECE, rows, c0 % PIECE:c0 % PIECE + HEAD]

    row_blocks = [slice(r0, r0 + OUT_ROWS) for r0 in range(0, TILE, OUT_ROWS)]

    def project_next(pieces):
        for p in pieces:
            for rows in row_blocks:
                nxt[p, rows, :] = _dot(hn_ref[rows, :], w_refs[p][0])
                yield

    def queries():
        for p in range(D_C // PIECE):
            for rows in row_blocks:
                q_scr[rows, p * PIECE:(p + 1) * PIECE] = _silu_tanh(cur[p, rows, :]) * (HEAD ** -0.5)
                yield

    _interleave([queries(), project_next(range(PIECES_EARLY))])

    r_io = lax.broadcasted_iota(jnp.int32, (CHUNK, CHUNK), 0)
    c_io = lax.broadcasted_iota(jnp.int32, (CHUNK, CHUNK), 1)
    tri = jnp.where(c_io <= r_io, 1.0, 0.0).astype(BF16)
    consts = (jnp.concatenate([tri, tri], axis=1), r_io, r_io ^ c_io)

    for h in range(N_HEADS_C):
        for d in range(2):
            if has_state_in:
                s_scr[2 * h + d] = s0_ref[0, 0, d, h].T
            else:
                s_scr[2 * h + d] = jnp.zeros((HEAD, HEAD), F32)

    def body(it, carry):
        s = it // cps
        j = it % cps
        gens, where = [], []
        for h in range(N_HEADS_C):
            for d in range(2):
                backward = d == 1
                ch = s * cps + ((cps - 1 - j) if backward else j)
                rows = pl.ds(pl.multiple_of(ch * CHUNK, CHUNK), CHUNK)
                s_t = s_scr[2 * h + d]
                if not has_state_in:
                    s_t = jnp.where(j == 0, 0.0, s_t)
                gens.append(_hgrn_chunk(q_scr[rows, h * HEAD:(h + 1) * HEAD], head_cols(1, h, rows).astype(BF16),
                                        head_cols(2 + d, h, rows), lb_ref[d, 0, :, h * HEAD:(h + 1) * HEAD],
                                        s_t, consts, backward))
                where.append((h, d, rows))
        results = _interleave(gens)
        for (h, d, rows), (o, s_new) in zip(where, results):
            (ob_scr if d == 1 else of_scr)[rows, h * HEAD:(h + 1) * HEAD] = o
            s_scr[2 * h + d] = s_new
            if not has_state_in:
                sout_ref[s, 0, d, h] = s_new.T
        return carry

    lax.fori_loop(0, n_chunks, body, 0)

    def outputs():
        for h in range(N_HEADS_C):
            cols = slice(h * HEAD, (h + 1) * HEAD)
            for rows in row_blocks:
                o = of_scr[rows, cols] + ob_scr[rows, cols]
                y = _rms(o) * gn_ref[0, :, cols] * _silu_tanh(head_cols(4, h, rows))
                y_ref[rows, cols] = y.astype(y_ref.dtype)
                yield

    _interleave([outputs(), project_next(range(PIECES_EARLY, N_PIECES))])


def _hgrn_kernel(hn_ref, *refs, ctx_seqs):
    w_refs = refs[:N_PIECES]
    lb_ref, gn_ref, s0_ref, y_ref, sout_ref, q_scr, of_scr, ob_scr, s_scr, proj_scr = refs[N_PIECES:]
    i = pl.program_id(0)

    @pl.when(i == 0)
    def _():
        for p in range(N_PIECES):
            proj_scr[0, p] = _dot(hn_ref[...], w_refs[p][0])

    slot = (i + 1) % 2
    cur, nxt = proj_scr.at[slot], proj_scr.at[1 - slot]
    shared = (cur, nxt, hn_ref, w_refs, lb_ref, gn_ref)
    scratch = (q_scr, of_scr, ob_scr, s_scr)

    @pl.when((i >= 1) & (i <= STREAM_TILES))
    def _():
        _hgrn_phase(*shared, None, y_ref, sout_ref, *scratch, ctx_seqs)

    @pl.when(i > STREAM_TILES)
    def _():
        _hgrn_phase(*shared, s0_ref, y_ref, None, *scratch, 1)


def _hgrn_mix(h_bf, w_in_bf, lower_bound, gn_g, state_in, prev_states, layer, ctx_seqs):
    n_tiles = ROWS // TILE
    tile = lambda i: jnp.maximum(i - 1, 0)
    in_specs = [
        pl.BlockSpec((TILE, D_MODEL), lambda i: (jnp.minimum(i, n_tiles - 1), 0)),
        *[pl.BlockSpec((1, D_MODEL, PIECE), lambda i, p=p: (layer, 0, PROJ_AB // PIECE + p)) for p in range(N_PIECES)],
        pl.BlockSpec((2, 1, 1, D_C), lambda i: (0, layer, 0, 0)),
        _layer_vec_spec(layer, D_C),
        pl.BlockSpec((1, 1, 2, N_HEADS_C, HEAD, HEAD),
                     lambda i: (jnp.maximum(tile(i) - STREAM_TILES, 0), layer, 0, 0, 0, 0)),
    ]
    args = [h_bf, *[w_in_bf] * N_PIECES, lower_bound, gn_g, state_in]
    aliases = {}
    if prev_states is not None:
        in_specs.append(pl.BlockSpec(memory_space=pl.ANY))
        args.append(prev_states)
        aliases[len(args) - 1] = 1
    kern = functools.partial(_hgrn_kernel, ctx_seqs=ctx_seqs)
    if prev_states is not None:
        kern = functools.partial(_drop_operand, kern, len(args) - 1)
    y, states = pl.pallas_call(
        kern,
        grid=(n_tiles + 1,),
        in_specs=in_specs,
        out_specs=[
            pl.BlockSpec((TILE, D_C), lambda i: (tile(i), 0)),
            pl.BlockSpec((ctx_seqs, 1, 2, N_HEADS_C, HEAD, HEAD),
                         lambda i: (jnp.minimum(tile(i), STREAM_TILES - 1), layer, 0, 0, 0, 0)),
        ],
        out_shape=[
            jax.ShapeDtypeStruct((ROWS, D_C), BF16),
            jax.ShapeDtypeStruct((STREAM_TILES * ctx_seqs, DEPTH, 2, N_HEADS_C, HEAD, HEAD), F32),
        ],
        scratch_shapes=[pltpu.VMEM((TILE, D_C), F32), pltpu.VMEM((TILE, D_C), F32),
                        pltpu.VMEM((TILE, D_C), F32), pltpu.VMEM((2 * N_HEADS_C, HEAD, HEAD), F32),
                        pltpu.VMEM((2, N_PIECES, TILE, PIECE), F32)],
        input_output_aliases=aliases,
        compiler_params=_params("arbitrary"),
        name="hgrn_mix",
    )(*args)
    return y, states


def _drop_operand(kern, index, *refs):
    return kern(*refs[:index], *refs[index + 1:])


ROUTER_LANES = 128
SUB = 128
N_SLAB = TILE // 128
OUT_ROWS = 256


def _route(lg, bias_ref):
    shape = lg[0].shape
    mx = functools.reduce(jnp.maximum, lg)
    ex = [jnp.exp(l - mx) for l in lg]
    den = functools.reduce(jnp.add, ex)
    sc = [e / den for e in ex]
    biased = [sc[e] + bias_ref[e:e + 1, :] for e in range(N_EXPERTS)]

    best = jnp.zeros(shape, jnp.int32)
    best_val = None
    for g in range(N_GROUPS):
        mem = biased[g * EXPERTS_PER_GROUP:(g + 1) * EXPERTS_PER_GROUP]
        pair = None
        for a in range(EXPERTS_PER_GROUP):
            for b in range(a + 1, EXPERTS_PER_GROUP):
                sm = mem[a] + mem[b]
                pair = sm if pair is None else jnp.maximum(pair, sm)
        if best_val is None:
            best_val = pair
        else:
            upd = pair > best_val
            best = jnp.where(upd, g, best)
            best_val = jnp.where(upd, pair, best_val)

    masked = [jnp.where(best == (e // EXPERTS_PER_GROUP), biased[e], MASK_VALUE) for e in range(N_EXPERTS)]
    v1, i1 = masked[0], jnp.zeros(shape, jnp.int32)
    for e in range(1, N_EXPERTS):
        upd = masked[e] > v1
        i1 = jnp.where(upd, e, i1)
        v1 = jnp.where(upd, masked[e], v1)
    v2 = jnp.full(shape, -jnp.inf, F32)
    i2 = jnp.full(shape, -1, jnp.int32)
    for e in range(N_EXPERTS):
        cand = jnp.where(i1 == e, -jnp.inf, masked[e])
        upd = cand > v2
        i2 = jnp.where(upd, e, i2)
        v2 = jnp.where(upd, cand, v2)
    w1 = functools.reduce(jnp.add, [jnp.where(i1 == e, sc[e], 0.0) for e in range(N_EXPERTS)])
    w2 = functools.reduce(jnp.add, [jnp.where(i2 == e, sc[e], 0.0) for e in range(N_EXPERTS)])
    inv = 1.0 / (w1 + w2)
    gates = [(jnp.where(i1 == e, w1, 0.0) + jnp.where(i2 == e, w2, 0.0)) * inv for e in range(N_EXPERTS)]
    return gates, best


def _plan(best):
    r_io = lax.broadcasted_iota(jnp.int32, (128, 128), 0)
    c_io = lax.broadcasted_iota(jnp.int32, (128, 128), 1)
    before = jnp.where(r_io < c_io, 1.0, 0.0).astype(BF16)
    ranks, counts = [], []
    for g in range(N_GROUPS):
        member = jnp.where(best == g, 1.0, 0.0)
        within = _dot(member.astype(BF16), before)
        tot = jnp.sum(member, axis=1, keepdims=True)
        run = jnp.zeros((1, 1), F32)
        offs = []
        for c in range(N_SLAB):
            offs.append(run)
            run = run + tot[c:c + 1]
        ranks.append((member, within + jnp.concatenate(offs, axis=0)))
        counts.append(run)
    subs = [jnp.floor((n + (SUB - 1.0)) * (1.0 / SUB)) for n in counts]
    start = jnp.zeros((1, 1), F32)
    starts = []
    pos = jnp.zeros(best.shape, F32)
    for g in range(N_GROUPS):
        starts.append(start)
        member, rank = ranks[g]
        pos = pos + member * (start * SUB + rank)
        start = start + subs[g]
    return pos.astype(jnp.int32), jnp.concatenate(starts + subs, axis=0)


def _outproj_chunk(rows, x_refs, yab_ref, yc_ref, w_ref, xo_ref, h_ref, mod, g_norm, wr_split):
    g1, sh2, sc2 = mod
    y = _dot(yab_ref[rows, :], w_ref[0, 0:D_A + D_B, :]) + _dot(yc_ref[rows, :], w_ref[0, D_A + D_B:, :])
    yield
    x = _stream_rows(x_refs, pl.program_id(0), rows) + g1 * y
    xo_ref[rows, :] = x
    h = _rms(x) * g_norm * (1.0 + sc2) + sh2
    h_hi = h.astype(BF16)
    h_ref[rows, :] = h_hi
    h_lo = (h - h_hi.astype(F32)).astype(BF16)
    yield
    prod = _dot(jnp.concatenate([h_hi, h_lo], axis=0), wr_split)
    logits = (prod[0:OUT_ROWS, 0:ROUTER_LANES] + prod[0:OUT_ROWS, ROUTER_LANES:]
              + prod[OUT_ROWS:, 0:ROUTER_LANES])
    yield
    return logits.T


def _outproj_kernel(*refs):
    (yab_ref, yc_ref, mod_ref, g_ref, w_ref, wr_ref, br_ref, xo_ref, h_ref, gates_ref, pos_ref,
     tab_ref) = refs[-12:]
    x_refs = refs[:-12]
    m = mod_ref[0, 0]
    mod = (m[:, 2 * D_MODEL:3 * D_MODEL], m[:, 3 * D_MODEL:4 * D_MODEL], m[:, 4 * D_MODEL:5 * D_MODEL])
    wr = wr_ref[...]
    wr_hi = wr.astype(BF16)
    wr_lo = (wr - wr_hi.astype(F32)).astype(BF16)
    wr_split = jnp.concatenate([wr_hi, wr_lo], axis=1)
    chunks = [_outproj_chunk(pl.ds(c * OUT_ROWS, OUT_ROWS), x_refs, yab_ref, yc_ref, w_ref, xo_ref, h_ref, mod,
                             g_ref[0], wr_split) for c in range(TILE // OUT_ROWS)]
    logits_t = _interleave(chunks)
    lg = [jnp.concatenate([lt[e:e + 1, k * 128:(k + 1) * 128] for lt in logits_t for k in range(OUT_ROWS // 128)],
                          axis=0) for e in range(N_EXPERTS)]
    gates, best = _route(lg, br_ref)
    for e in range(N_EXPERTS):
        gates_ref[0, e] = gates[e]
    pos, tab = _plan(best)
    pos_ref[0] = pos
    tab_ref[0] = jnp.broadcast_to(tab, (8, 128)).astype(jnp.int32)


def _outproj(x, y_ab, y_c, mod, norm_g, w_out_bf, w_router_pad, b_router, layer):
    n_blk = ROWS // TILE
    x_specs, x_args = _stream_specs(x, lambda i: i)
    return pl.pallas_call(
        _outproj_kernel,
        grid=(n_blk,),
        in_specs=x_specs + [
            pl.BlockSpec((TILE, D_A + D_B), lambda i: (i, 0)),
            pl.BlockSpec((TILE, D_C), lambda i: (i, 0)),
            _mod_spec(layer, lambda i: i),
            _layer_vec_spec(layer, D_MODEL),
            pl.BlockSpec((1, D_MODEL, D_MODEL), lambda i: (layer, 0, 0)),
            pl.BlockSpec((D_MODEL, ROUTER_LANES), lambda i: (0, 0)),
            pl.BlockSpec((N_EXPERTS, 1), lambda i: (0, 0)),
        ],
        out_specs=[
            pl.BlockSpec((TILE, D_MODEL), lambda i: (i, 0)),
            pl.BlockSpec((TILE, D_MODEL), lambda i: (i, 0)),
            pl.BlockSpec((1, N_EXPERTS, N_SLAB, 128), lambda i: (i, 0, 0, 0)),
            pl.BlockSpec((1, N_SLAB, 128), lambda i: (i, 0, 0)),
            pl.BlockSpec((1, 8, 128), lambda i: (i, 0, 0)),
        ],
        out_shape=[
            jax.ShapeDtypeStruct((ROWS, D_MODEL), F32),
            jax.ShapeDtypeStruct((ROWS, D_MODEL), BF16),
            jax.ShapeDtypeStruct((n_blk, N_EXPERTS, N_SLAB, 128), F32),
            jax.ShapeDtypeStruct((n_blk, N_SLAB, 128), jnp.int32),
            jax.ShapeDtypeStruct((n_blk, 8, 128), jnp.int32),
        ],
        compiler_params=_params("arbitrary"),
        name="outproj_router",
    )(*x_args, y_ab, y_c, mod, norm_g, w_out_bf, w_router_pad, b_router)


MOE_VMEM_LIMIT = 58 * 1024 * 1024


def _moe_kernel(tab_ref, x_ref, h_ref, pos_ref, gates_ref, mod_ref, wg_ref, wu_ref, wd_ref, fg_ref, *rest, final):
    blk = pl.program_id(0)
    grp = pl.program_id(1)
    if final:
        (out_ref,) = rest
    else:
        ng_ref, nu_ref, nd_ref, out_ref, ng_out, nu_out, nd_out = rest
        ng_out[0] = ng_ref[0, 0].astype(BF16)
        nu_out[0] = nu_ref[0, 0].astype(BF16)
        nd_out[0] = nd_ref[0, 0].astype(BF16)

    @pl.when(grp == 0)
    def _():
        out_ref[...] = jnp.zeros_like(out_ref)

    first = tab_ref[blk, grp]
    count = tab_ref[blk, N_GROUPS + grp]
    gates = gates_ref[0, 0]
    g_hi = gates.astype(BF16)
    g_lo = (gates - g_hi.astype(F32)).astype(BF16)
    g_split = jnp.concatenate([g_hi, g_lo], axis=0)

    def sorted_rows(s, n_sub):
        rows = n_sub * SUB
        r_io = lax.broadcasted_iota(jnp.int32, (rows, TILE), 0)
        sel = jnp.where(r_io == pos_ref[...] - (first + s) * SUB, 1.0, 0.0).astype(BF16)
        xs = _dot(sel, h_ref[...]).astype(BF16)
        gs = _dot_nt(sel, g_split)
        y = jnp.zeros((rows, D_MODEL), F32)
        for e in range(EXPERTS_PER_GROUP):
            gate = gs[:, e:e + 1] + gs[:, EXPERTS_PER_GROUP + e:EXPERTS_PER_GROUP + e + 1]
            act = _silu(_dot(xs, wg_ref[e])) * _dot(xs, wu_ref[e]) * gate
            y = y + _dot(act.astype(BF16), wd_ref[e])
        out_ref[...] += _dot_tn(sel, y.astype(BF16))

    def pair(p, carry):
        sorted_rows(2 * p, 2)
        return carry

    lax.fori_loop(0, count // 2, pair, 0)

    @pl.when(count % 2 == 1)
    def _():
        sorted_rows(count - 1, 1)

    @pl.when(grp == N_GROUPS - 1)
    def _():
        g2 = mod_ref[0, 0][:, 5 * D_MODEL:6 * D_MODEL]
        x = x_ref[...] + g2 * out_ref[...]
        if final:
            x = _rms(x) * fg_ref[...]
        out_ref[...] = x


def _moe(x, h, pos, tab, gates, mod, w_bf, w_next, final_g, layer):
    final = w_next is None
    n_blk = ROWS // TILE
    steps = n_blk * N_GROUPS
    halves = steps // N_EXPERTS
    in_specs = [
        pl.BlockSpec((TILE, D_MODEL), lambda i, g, t: (i, 0)),
        pl.BlockSpec((TILE, D_MODEL), lambda i, g, t: (i, 0)),
        pl.BlockSpec((1, TILE), lambda i, g, t: (0, i)),
        pl.BlockSpec((1, 1, EXPERTS_PER_GROUP, TILE), lambda i, g, t: (i, g, 0, 0)),
        pl.BlockSpec((1, 1, 1, N_MOD * D_MODEL), lambda i, g, t: (layer, _cond_of_tile(i), 0, 0)),
        pl.BlockSpec((EXPERTS_PER_GROUP, D_MODEL, D_EXPERT), lambda i, g, t: (g, 0, 0)),
        pl.BlockSpec((EXPERTS_PER_GROUP, D_MODEL, D_EXPERT), lambda i, g, t: (g, 0, 0)),
        pl.BlockSpec((EXPERTS_PER_GROUP, D_EXPERT, D_MODEL), lambda i, g, t: (g, 0, 0)),
        pl.BlockSpec((1, D_MODEL), lambda i, g, t: (0, 0)),
    ]
    out_specs = [pl.BlockSpec((TILE, D_MODEL), lambda i, g, t: (i, 0))]
    out_shape = [jax.ShapeDtypeStruct((ROWS, D_MODEL), F32)]
    args = [tab, x, h, pos, gates, mod, *w_bf, final_g]
    if not final:
        def slab(i, g):
            step = i * N_GROUPS + g
            return step // halves, step % halves

        for w in w_next:
            rows = w.shape[2] // halves
            in_specs.append(pl.BlockSpec((1, 1, rows, w.shape[3]),
                                         lambda i, g, t: (layer + 1, *slab(i, g), 0)))
            out_specs.append(pl.BlockSpec((1, rows, w.shape[3]), lambda i, g, t: (*slab(i, g), 0)))
            out_shape.append(jax.ShapeDtypeStruct(w.shape[1:], BF16))
            args.append(w)
    res = pl.pallas_call(
        functools.partial(_moe_kernel, final=final),
        grid_spec=pltpu.PrefetchScalarGridSpec(num_scalar_prefetch=1, grid=(n_blk, N_GROUPS),
                                               in_specs=in_specs, out_specs=out_specs),
        out_shape=out_shape,
        compiler_params=pltpu.CompilerParams(dimension_semantics=("arbitrary", "arbitrary"),
                                             vmem_limit_bytes=MOE_VMEM_LIMIT),
        name="moe",
    )(*args)
    return res[0], (None if final else tuple(res[1:]))


def kernel(x_prompt, x_sample, state_hgrn, c, c_ctx, norm_mix_g, norm_ffn_g, final_norm_g, w_ada, b_ada, w_in, w_out, conv_a_w, conv_b_w, conv_b_b, ln_b_g, ln_b_b, lb_logits, gnorm_c_g, w_router, b_router, w_gate, w_up, w_down):
    batch, seq, _ = x_prompt.shape
    dec_batch, dec_seq, _ = x_sample.shape
    assert batch * seq == STREAM_ROWS and dec_batch * dec_seq == STREAM_ROWS
    assert dec_seq == TILE and TILE % seq == 0 and dec_batch + 1 <= 8

    vec = lambda a: a.reshape(DEPTH, 1, a.shape[-1])
    w_in_bf, w_out_bf = w_in.astype(BF16), w_out.astype(BF16)
    w_experts = (w_gate, w_up, w_down)
    w_bf = tuple(w[0].astype(BF16) for w in w_experts)
    w_router_pad = jnp.pad(w_router, ((0, 0), (0, ROUTER_LANES - N_EXPERTS)))
    b_router2 = b_router.reshape(N_EXPERTS, 1)
    final_g = final_norm_g.reshape(1, D_MODEL)
    norm_mix, norm_ffn, gn_g = vec(norm_mix_g), vec(norm_ffn_g), vec(gnorm_c_g)
    cb_b, ln_g, ln_b = vec(conv_b_b), vec(ln_b_g), vec(ln_b_b)

    lower_bound = _lower_bound(lb_logits).reshape(2, DEPTH, 1, D_C)
    cond8 = jnp.concatenate([c_ctx[None, :], c, jnp.zeros((8 - 1 - dec_batch, D_MODEL), F32)], axis=0)
    mod = _adaln(cond8, w_ada, b_ada).reshape(DEPTH, 8, 1, N_MOD * D_MODEL)

    x = (x_prompt.reshape(STREAM_ROWS, D_MODEL), x_sample.reshape(STREAM_ROWS, D_MODEL))
    states = None
    for l in range(DEPTH):
        y_ab, h_mix = _conv_mix(x, mod, norm_mix, w_in_bf, conv_a_w, conv_b_w, cb_b, ln_g, ln_b, l, seq)
        y_c, states = _hgrn_mix(h_mix, w_in_bf, lower_bound, gn_g, state_hgrn, states, l, TILE // seq)
        x, h, gates, pos, tab = _outproj(x, y_ab, y_c, mod, norm_ffn, w_out_bf, w_router_pad, b_router2, l)
        x, w_bf = _moe(x, h, pos.reshape(1, ROWS), tab[:, :, 0],
                       gates.reshape(ROWS // TILE, N_GROUPS, EXPERTS_PER_GROUP, TILE), mod,
                       w_bf, w_experts if l + 1 < DEPTH else None, final_g, l)
    return (x[:STREAM_ROWS].reshape(batch, seq, D_MODEL), x[STREAM_ROWS:].reshape(dec_batch, dec_seq, D_MODEL),
            states)
```

```python
import functools

import jax
import jax.numpy as jnp
from jax import lax
from jax.experimental import pallas as pl
from jax.experimental.pallas import tpu as pltpu

F32 = jnp.float32
BF16 = jnp.bfloat16

D_MODEL = 1024
DEPTH = 4
GRID_W = 64
D_A = 256
D_B = 256
D_C = 512
N_HEADS_C = 4
HEAD = 128
CONV_A = 3
CONV_B = 31
N_EXPERTS = 16
N_GROUPS = 4
EXPERTS_PER_GROUP = 4
D_EXPERT = 512
N_MOD = 6
D_IN = 3840
EPS = 1e-6
MASK_VALUE = -1e9
LOG2E = 1.4426950408889634
LANES = 128
SUBLANES = 8

STREAM_ROWS = 4096
ROWS = 2 * STREAM_ROWS
TILE = 1024
STREAM_TILES = STREAM_ROWS // TILE
CHUNK = 128
N_LEVELS = 7
VMEM_LIMIT = 52 * 1024 * 1024


def _params(*sem, vmem_limit=VMEM_LIMIT):
    return pltpu.CompilerParams(dimension_semantics=sem, vmem_limit_bytes=vmem_limit)


def _dot(a, b):
    return jnp.dot(a, b, preferred_element_type=F32)


def _dot_nt(a, b):
    return lax.dot_general(a, b, (((1,), (1,)), ((), ())), preferred_element_type=F32)


def _dot_tn(a, b):
    return lax.dot_general(a, b, (((0,), (0,)), ((), ())), preferred_element_type=F32)


def _sigmoid(x):
    return 1.0 / (1.0 + jnp.exp2(x * (-LOG2E)))


def _silu(x):
    return x * _sigmoid(x)


def _silu_tanh(x):
    return x * (0.5 * jnp.tanh(0.5 * x) + 0.5)


def _rms(x):
    return x * lax.rsqrt(jnp.mean(x * x, axis=-1, keepdims=True) + EPS)


def _cond_of_tile(i):
    return jnp.maximum(i - (STREAM_TILES - 1), 0)


def _lower_bound_kernel(lg_ref, out_ref):
    for d in range(2):
        x = lg_ref[d]
        mx = jnp.max(x, axis=0, keepdims=True)
        e = jnp.exp(x - mx)
        p = e / jnp.sum(e, axis=0, keepdims=True)
        run = jnp.zeros_like(p[0:1])
        rows = []
        for l in range(DEPTH):
            run = run + p[l:l + 1]
            rows.append(run - p[0:1])
        out_ref[d] = jnp.concatenate(rows, axis=0)


def _lower_bound(lb_logits):
    return pl.pallas_call(
        _lower_bound_kernel,
        out_shape=jax.ShapeDtypeStruct((2, DEPTH, D_C), F32),
        name="lower_bound",
    )(lb_logits)


def _adaln_kernel(cond_ref, w_ref, b_ref, out_ref):
    a = _silu(cond_ref[...]).astype(BF16)
    out_ref[0] = _dot(a, w_ref[0].astype(BF16)) + b_ref[0]


def _adaln(cond8, w_ada, b_ada):
    tn = 1536
    nt = (N_MOD * D_MODEL) // tn
    return pl.pallas_call(
        _adaln_kernel,
        grid=(DEPTH, nt),
        in_specs=[
            pl.BlockSpec((8, D_MODEL), lambda l, j: (0, 0)),
            pl.BlockSpec((1, D_MODEL, tn), lambda l, j: (l, 0, j)),
            pl.BlockSpec((1, 1, tn), lambda l, j: (l, 0, j)),
        ],
        out_specs=pl.BlockSpec((1, 8, tn), lambda l, j: (l, 0, j)),
        out_shape=jax.ShapeDtypeStruct((DEPTH, 8, N_MOD * D_MODEL), F32),
        compiler_params=_params("arbitrary", "arbitrary"),
        name="adaln",
    )(cond8, w_ada, b_ada.reshape(DEPTH, 1, N_MOD * D_MODEL))


def _mod_spec(layer, tile_of):
    return pl.BlockSpec((1, 1, 1, N_MOD * D_MODEL),
                        lambda *g: (layer, _cond_of_tile(tile_of(*g)), 0, 0))


def _layer_vec_spec(layer, width):
    return pl.BlockSpec((1, 1, width), lambda *g: (layer, 0, 0))


def _stream_rows(x_refs, tile, rows=slice(None)):
    if len(x_refs) == 1:
        return x_refs[0][rows, :]
    return jnp.where(tile < STREAM_TILES, x_refs[0][rows, :], x_refs[1][rows, :])


def _stream_specs(x, tile_of):
    if not isinstance(x, tuple):
        return [pl.BlockSpec((TILE, D_MODEL), lambda *g: (tile_of(*g), 0))], [x]
    return ([pl.BlockSpec((TILE, D_MODEL), lambda *g: (jnp.minimum(tile_of(*g), STREAM_TILES - 1), 0)),
             pl.BlockSpec((TILE, D_MODEL), lambda *g: (jnp.maximum(tile_of(*g) - STREAM_TILES, 0), 0))],
            list(x))


PAD_B = 16
CONV_ROWS = 64


PROJ_AB = 3 * D_A + 2 * D_B


def _conv_body(p_ref, wa_ref, wb_ref, bb_ref, lng_ref, lnb_ref, out_ref, u_scr, c_scr, sh_scr, seq_len):
    latent = seq_len == GRID_W
    a_b = p_ref[:, 0:D_A]
    z = p_ref[:, D_A:2 * D_A] * p_ref[:, 2 * D_A:3 * D_A]
    pos = lax.broadcasted_iota(jnp.int32, (TILE, D_A), 0) % seq_len
    z_prev = jnp.where(pos == 0, 0.0, pltpu.roll(z, 1, axis=0))
    z_next = jnp.where(pos == seq_len - 1, 0.0, pltpu.roll(z, TILE - 1, axis=0))
    wa = wa_ref[0]
    y_a = a_b * (wa[0:1] * z_prev + wa[1:2] * z + wa[2:3] * z_next)
    out_ref[:, 0:D_A] = y_a.astype(out_ref.dtype)

    u = p_ref[:, 3 * D_A:3 * D_A + D_B] * _sigmoid(p_ref[:, 3 * D_A + D_B:3 * D_A + 2 * D_B])
    if latent:
        n_r = TILE // GRID_W
        u_scr[0:TILE, :] = u

        def row_body(r, carry):
            acc = jnp.zeros((GRID_W, D_B), F32)
            for rp in range(n_r):
                w = wb_ref[0, pl.ds(CONV_B // 2 + rp - r, 1), :]
                acc = acc + w * u_scr[rp * GRID_W:(rp + 1) * GRID_W, :]
            c_scr[pl.ds(pl.multiple_of(r * GRID_W, GRID_W), GRID_W), :] = acc
            return carry

        lax.fori_loop(0, n_r, row_body, 0)
    else:
        n_seq = TILE // seq_len
        padded = seq_len + 2 * PAD_B
        zeros = jnp.zeros((PAD_B, D_B), F32)
        for s in range(n_seq):
            u_scr[s * padded:s * padded + PAD_B, :] = zeros
            u_scr[s * padded + PAD_B:s * padded + PAD_B + seq_len, :] = u[s * seq_len:(s + 1) * seq_len]
            u_scr[s * padded + PAD_B + seq_len:(s + 1) * padded, :] = zeros
        u_scr[n_seq * padded:n_seq * padded + 8, :] = jnp.zeros((8, D_B), F32)
        for b in range(1, 8):
            sh_scr[b - 1] = u_scr[b:b + n_seq * padded, :]
        for s in range(n_seq):
            for c in range(seq_len // CONV_ROWS):
                base = s * padded + c * CONV_ROWS
                for l0 in range(0, D_B, LANES):
                    lanes = slice(l0, l0 + LANES)
                    acc = jnp.zeros((CONV_ROWS, LANES), F32)
                    for j in range(CONV_B):
                        off = j + PAD_B - CONV_B // 2
                        lo = base + off - off % 8
                        if off % 8 == 0:
                            tap = u_scr[lo:lo + CONV_ROWS, lanes]
                        else:
                            tap = sh_scr[off % 8 - 1, lo:lo + CONV_ROWS, lanes]
                        acc = acc + wb_ref[0, j:j + 1, lanes] * tap
                    c_scr[s * seq_len + c * CONV_ROWS:s * seq_len + (c + 1) * CONV_ROWS, lanes] = acc

    v = c_scr[...] + bb_ref[0]
    mu = jnp.mean(v, axis=-1, keepdims=True)
    d = v - mu
    var = jnp.mean(d * d, axis=-1, keepdims=True)
    ln = d * lax.rsqrt(var + EPS) * lng_ref[0] + lnb_ref[0]
    out_ref[:, D_A:D_A + D_B] = _silu(ln).astype(out_ref.dtype)


def _conv_kernel(*refs, ctx_len):
    (mod_ref, g_ref, w_ref, wa_ref, wb_ref, bb_ref, lng_ref, lnb_ref, out_ref, h_ref,
     p_scr, u_scr, c_scr, sh_scr, w_scr) = refs[-15:]
    x_refs = refs[:-15]
    i = pl.program_id(0)

    @pl.when(i == 0)
    def _():
        w_scr[...] = w_ref[0].astype(BF16)

    m = mod_ref[0, 0]
    for r0 in range(0, TILE, OUT_ROWS):
        rows = slice(r0, r0 + OUT_ROWS)
        h = _rms(_stream_rows(x_refs, i, rows)) * g_ref[0] * (1.0 + m[:, D_MODEL:2 * D_MODEL]) + m[:, 0:D_MODEL]
        h = h.astype(BF16)
        h_ref[rows, :] = h
        p_scr[rows, :] = _dot(h, w_scr[...])
    mix = (p_scr, wa_ref, wb_ref, bb_ref, lng_ref, lnb_ref, out_ref, u_scr, c_scr, sh_scr)

    @pl.when(i < STREAM_TILES)
    def _():
        _conv_body(*mix, ctx_len)

    @pl.when(i >= STREAM_TILES)
    def _():
        _conv_body(*mix, GRID_W)


def _conv_mix(x, mod, norm_g, w_in, conv_a_w, conv_b_w, conv_b_b, ln_g, ln_b, layer, ctx_len):
    pad_rows = (TILE // ctx_len) * (ctx_len + 2 * PAD_B)
    x_specs, x_args = _stream_specs(x, lambda i: i)
    return pl.pallas_call(
        functools.partial(_conv_kernel, ctx_len=ctx_len),
        grid=(ROWS // TILE,),
        in_specs=x_specs + [
            _mod_spec(layer, lambda i: i),
            _layer_vec_spec(layer, D_MODEL),
            pl.BlockSpec((1, D_MODEL, PROJ_AB), lambda i: (layer, 0, 0), pipeline_mode=pl.Buffered(1)),
            pl.BlockSpec((1, CONV_A, D_A), lambda i: (layer, 0, 0)),
            pl.BlockSpec((1, CONV_B, D_B), lambda i: (layer, 0, 0)),
            _layer_vec_spec(layer, D_B), _layer_vec_spec(layer, D_B), _layer_vec_spec(layer, D_B),
        ],
        out_specs=[pl.BlockSpec((TILE, D_A + D_B), lambda i: (i, 0)),
                   pl.BlockSpec((TILE, D_MODEL), lambda i: (i, 0))],
        out_shape=[jax.ShapeDtypeStruct((ROWS, D_A + D_B), BF16), jax.ShapeDtypeStruct((ROWS, D_MODEL), BF16)],
        scratch_shapes=[pltpu.VMEM((TILE, PROJ_AB), F32), pltpu.VMEM((pad_rows + 8, D_B), F32),
                        pltpu.VMEM((TILE, D_B), F32), pltpu.VMEM((7, pad_rows, D_B), F32),
                        pltpu.VMEM((D_MODEL, PROJ_AB), BF16)],
        compiler_params=_params("arbitrary"),
        name="conv_mix",
    )(*x_args, mod, norm_g, w_in, conv_a_w, conv_b_w, conv_b_b, ln_g, ln_b)


def _ref_rows(c, m, ref_pos, pos_in_block):
    size = 2 * m
    if size >= 8:
        pieces = [jnp.broadcast_to(c[b * size + ref_pos:b * size + ref_pos + 1, :], (size, HEAD))
                  for b in range(CHUNK // size)]
        return pieces[0] if len(pieces) == 1 else jnp.concatenate(pieces, axis=0)
    if size == 4:
        c3 = c.reshape(CHUNK // 8, 8, HEAD)
        sub = lax.broadcasted_iota(jnp.int32, c3.shape, 1)
        lo = jnp.broadcast_to(c3[:, ref_pos:ref_pos + 1, :], c3.shape)
        hi = jnp.broadcast_to(c3[:, 4 + ref_pos:5 + ref_pos, :], c3.shape)
        return jnp.where(sub < 4, lo, hi).reshape(CHUNK, HEAD)
    out = c
    for p in range(size):
        shift = p - ref_pos
        if shift == 0:
            continue
        out = jnp.where(pos_in_block == p, pltpu.roll(c, shift % CHUNK, axis=0), out)
    return out


def _hgrn_chunk(q, v_bf, x, lb, s_t, consts, backward):
    tri2, row, xor_rc = consts
    sg = _sigmoid(x)
    one_m = 1.0 - lb
    lf = jnp.log2(lb + one_m * sg)
    kk = one_m * (1.0 - sg)
    lf_hi = lf.astype(BF16)
    lf_lo = (lf - lf_hi.astype(F32)).astype(BF16)
    yield
    cb = _dot(tri2, jnp.concatenate([lf_hi, lf_lo], axis=0))
    total = cb[CHUNK - 1:CHUNK, :]
    c = cb - lf if backward else cb
    scores = _dot_nt(q.astype(BF16), kk.astype(BF16))
    yield

    for b in range(N_LEVELS):
        m = 1 << b
        if m < 8:
            g = _ref_rows(c, m, m if backward else m - 1, row & (2 * m - 1))
            e = jnp.exp2(-jnp.abs(c - g))
            qt, kt = (q * e).astype(BF16), (kk * e).astype(BF16)
        else:
            zeros = jnp.zeros((m, HEAD), F32)
            q_rows, k_rows = [], []
            for blk in range(CHUNK // (2 * m)):
                first = slice(2 * m * blk, 2 * m * blk + m)
                second = slice(2 * m * blk + m, 2 * m * (blk + 1))
                if backward:
                    g = c[2 * m * blk + m:2 * m * blk + m + 1, :]
                    q_rows += [q[first] * jnp.exp2(g - c[first]), zeros]
                    k_rows += [zeros, kk[second] * jnp.exp2(c[second] - g)]
                else:
                    g = c[2 * m * blk + m - 1:2 * m * blk + m, :]
                    q_rows += [zeros, q[second] * jnp.exp2(c[second] - g)]
                    k_rows += [kk[first] * jnp.exp2(g - c[first]), zeros]
            qt = jnp.concatenate(q_rows, axis=0).astype(BF16)
            kt = jnp.concatenate(k_rows, axis=0).astype(BF16)
        yield
        scores = jnp.where(xor_rc < m, scores, _dot_nt(qt, kt))
        yield
    col = row ^ xor_rc
    scores = jnp.where((row <= col) if backward else (row >= col), scores, 0.0).astype(BF16)

    if backward:
        q_dec = q * jnp.exp2(total - c)
        k_dec = kk * jnp.exp2(c)
    else:
        q_dec = q * jnp.exp2(c)
        k_dec = kk * jnp.exp2(total - c)
    q_dec, k_dec = q_dec.astype(BF16), k_dec.astype(BF16)
    yield
    o = _dot(scores, v_bf) + _dot_nt(q_dec, s_t.astype(BF16))
    s_new = s_t * jnp.exp2(total) + _dot_tn(v_bf, k_dec)
    return o, s_new


def _interleave(gens):
    results = [None] * len(gens)
    live = list(range(len(gens)))
    while live:
        still = []
        for i in live:
            try:
                next(gens[i])
                still.append(i)
            except StopIteration as stop:
                results[i] = stop.value
        live = still
    return results


PROJ_C = 5 * D_C
PIECE = 256
N_PIECES = PROJ_C // PIECE
PIECES_EARLY = 3
HGRN_VMEM_LIMIT = 56 * 1024 * 1024


def _hgrn_phase(cur, nxt, hn_ref, w_scr, lb_ref, gn_ref, s0_ref, y_ref, sout_ref,
                q_scr, of_scr, ob_scr, s_scr, n_seq):
    has_state_in = s0_ref is not None
    n_chunks = TILE // CHUNK
    cps = n_chunks // n_seq

    def head_cols(section, h, rows=slice(None)):
        c0 = section * D_C + h * HEAD
        return cur[c0 // PIECE, rows, c0 % PIECE:c0 % PIECE + HEAD]

    row_blocks = [slice(r0, r0 + OUT_ROWS) for r0 in range(0, TILE, OUT_ROWS)]

    def project_next(pieces):
        for p in pieces:
            for rows in row_blocks:
                nxt[p, rows, :] = _dot(hn_ref[rows, :], w_scr[p])
                yield

    def queries():
        for p in range(D_C // PIECE):
            for rows in row_blocks:
                q_scr[rows, p * PIECE:(p + 1) * PIECE] = _silu_tanh(cur[p, rows, :]) * (HEAD ** -0.5)
                yield

    _interleave([queries(), project_next(range(PIECES_EARLY))])

    r_io = lax.broadcasted_iota(jnp.int32, (CHUNK, CHUNK), 0)
    c_io = lax.broadcasted_iota(jnp.int32, (CHUNK, CHUNK), 1)
    tri = jnp.where(c_io <= r_io, 1.0, 0.0).astype(BF16)
    consts = (jnp.concatenate([tri, tri], axis=1), r_io, r_io ^ c_io)

    for h in range(N_HEADS_C):
        for d in range(2):
            if has_state_in:
                s_scr[2 * h + d] = s0_ref[0, 0, d, h].T
            else:
                s_scr[2 * h + d] = jnp.zeros((HEAD, HEAD), F32)

    def body(it, carry):
        s = it // cps
        j = it % cps
        gens, where = [], []
        for h in range(N_HEADS_C):
            for d in range(2):
                backward = d == 1
                ch = s * cps + ((cps - 1 - j) if backward else j)
                rows = pl.ds(pl.multiple_of(ch * CHUNK, CHUNK), CHUNK)
                s_t = s_scr[2 * h + d]
                if not has_state_in:
                    s_t = jnp.where(j == 0, 0.0, s_t)
                gens.append(_hgrn_chunk(q_scr[rows, h * HEAD:(h + 1) * HEAD], head_cols(1, h, rows).astype(BF16),
                                        head_cols(2 + d, h, rows), lb_ref[d, 0, :, h * HEAD:(h + 1) * HEAD],
                                        s_t, consts, backward))
                where.append((h, d, rows))
        results = _interleave(gens)
        for (h, d, rows), (o, s_new) in zip(where, results):
            (ob_scr if d == 1 else of_scr)[rows, h * HEAD:(h + 1) * HEAD] = o
            s_scr[2 * h + d] = s_new
            if not has_state_in:
                sout_ref[s, 0, d, h] = s_new.T
        return carry

    lax.fori_loop(0, n_chunks, body, 0)

    def outputs():
        for h in range(N_HEADS_C):
            cols = slice(h * HEAD, (h + 1) * HEAD)
            for rows in row_blocks:
                o = of_scr[rows, cols] + ob_scr[rows, cols]
                y = _rms(o) * gn_ref[0, :, cols] * _silu_tanh(head_cols(4, h, rows))
                y_ref[rows, cols] = y.astype(y_ref.dtype)
                yield

    _interleave([outputs(), project_next(range(PIECES_EARLY, N_PIECES))])


def _hgrn_kernel(hn_ref, *refs, ctx_seqs):
    w_refs = refs[:N_PIECES]
    lb_ref, gn_ref, s0_ref, y_ref, sout_ref, q_scr, of_scr, ob_scr, s_scr, proj_scr, w_scr = refs[N_PIECES:]
    i = pl.program_id(0)

    @pl.when(i == 0)
    def _():
        for p in range(N_PIECES):
            w_scr[p] = w_refs[p][0].astype(BF16)
            proj_scr[0, p] = _dot(hn_ref[...], w_scr[p])

    slot = (i + 1) % 2
    cur, nxt = proj_scr.at[slot], proj_scr.at[1 - slot]
    shared = (cur, nxt, hn_ref, w_scr, lb_ref, gn_ref)
    scratch = (q_scr, of_scr, ob_scr, s_scr)

    @pl.when((i >= 1) & (i <= STREAM_TILES))
    def _():
        _hgrn_phase(*shared, None, y_ref, sout_ref, *scratch, ctx_seqs)

    @pl.when(i > STREAM_TILES)
    def _():
        _hgrn_phase(*shared, s0_ref, y_ref, None, *scratch, 1)


def _hgrn_mix(h_bf, w_in, lower_bound, gn_g, state_in, prev_states, layer, ctx_seqs):
    n_tiles = ROWS // TILE
    tile = lambda i: jnp.maximum(i - 1, 0)
    in_specs = [
        pl.BlockSpec((TILE, D_MODEL), lambda i: (jnp.minimum(i, n_tiles - 1), 0)),
        *[pl.BlockSpec((1, D_MODEL, PIECE), lambda i, p=p: (layer, 0, PROJ_AB // PIECE + p),
                       pipeline_mode=pl.Buffered(1)) for p in range(N_PIECES)],
        pl.BlockSpec((2, 1, 1, D_C), lambda i: (0, layer, 0, 0)),
        _layer_vec_spec(layer, D_C),
        pl.BlockSpec((1, 1, 2, N_HEADS_C, HEAD, HEAD),
                     lambda i: (jnp.maximum(tile(i) - STREAM_TILES, 0), layer, 0, 0, 0, 0)),
    ]
    args = [h_bf, *[w_in] * N_PIECES, lower_bound, gn_g, state_in]
    aliases = {}
    if prev_states is not None:
        in_specs.append(pl.BlockSpec(memory_space=pl.ANY))
        args.append(prev_states)
        aliases[len(args) - 1] = 1
    kern = functools.partial(_hgrn_kernel, ctx_seqs=ctx_seqs)
    if prev_states is not None:
        kern = functools.partial(_drop_operand, kern, len(args) - 1)
    y, states = pl.pallas_call(
        kern,
        grid=(n_tiles + 1,),
        in_specs=in_specs,
        out_specs=[
            pl.BlockSpec((TILE, D_C), lambda i: (tile(i), 0)),
            pl.BlockSpec((ctx_seqs, 1, 2, N_HEADS_C, HEAD, HEAD),
                         lambda i: (jnp.minimum(tile(i), STREAM_TILES - 1), layer, 0, 0, 0, 0)),
        ],
        out_shape=[
            jax.ShapeDtypeStruct((ROWS, D_C), BF16),
            jax.ShapeDtypeStruct((STREAM_TILES * ctx_seqs, DEPTH, 2, N_HEADS_C, HEAD, HEAD), F32),
        ],
        scratch_shapes=[pltpu.VMEM((TILE, D_C), F32), pltpu.VMEM((TILE, D_C), F32),
                        pltpu.VMEM((TILE, D_C), F32), pltpu.VMEM((2 * N_HEADS_C, HEAD, HEAD), F32),
                        pltpu.VMEM((2, N_PIECES, TILE, PIECE), F32), pltpu.VMEM((N_PIECES, D_MODEL, PIECE), BF16)],
        input_output_aliases=aliases,
        compiler_params=_params("arbitrary", vmem_limit=HGRN_VMEM_LIMIT),
        name="hgrn_mix",
    )(*args)
    return y, states


def _drop_operand(kern, index, *refs):
    return kern(*refs[:index], *refs[index + 1:])


ROUTER_LANES = 128
SUB = 128
N_SLAB = TILE // LANES
OUT_ROWS = 256


def _route(lg, bias_ref):
    shape = lg[0].shape
    mx = functools.reduce(jnp.maximum, lg)
    ex = [jnp.exp(l - mx) for l in lg]
    den = functools.reduce(jnp.add, ex)
    sc = [e / den for e in ex]
    biased = [sc[e] + bias_ref[e:e + 1, :] for e in range(N_EXPERTS)]

    best = jnp.zeros(shape, jnp.int32)
    best_val = None
    for g in range(N_GROUPS):
        mem = biased[g * EXPERTS_PER_GROUP:(g + 1) * EXPERTS_PER_GROUP]
        pair = None
        for a in range(EXPERTS_PER_GROUP):
            for b in range(a + 1, EXPERTS_PER_GROUP):
                sm = mem[a] + mem[b]
                pair = sm if pair is None else jnp.maximum(pair, sm)
        if best_val is None:
            best_val = pair
        else:
            upd = pair > best_val
            best = jnp.where(upd, g, best)
            best_val = jnp.where(upd, pair, best_val)

    masked = [jnp.where(best == (e // EXPERTS_PER_GROUP), biased[e], MASK_VALUE) for e in range(N_EXPERTS)]
    v1, i1 = masked[0], jnp.zeros(shape, jnp.int32)
    for e in range(1, N_EXPERTS):
        upd = masked[e] > v1
        i1 = jnp.where(upd, e, i1)
        v1 = jnp.where(upd, masked[e], v1)
    v2 = jnp.full(shape, -jnp.inf, F32)
    i2 = jnp.full(shape, -1, jnp.int32)
    for e in range(N_EXPERTS):
        cand = jnp.where(i1 == e, -jnp.inf, masked[e])
        upd = cand > v2
        i2 = jnp.where(upd, e, i2)
        v2 = jnp.where(upd, cand, v2)
    w1 = functools.reduce(jnp.add, [jnp.where(i1 == e, sc[e], 0.0) for e in range(N_EXPERTS)])
    w2 = functools.reduce(jnp.add, [jnp.where(i2 == e, sc[e], 0.0) for e in range(N_EXPERTS)])
    inv = 1.0 / (w1 + w2)
    gates = [(jnp.where(i1 == e, w1, 0.0) + jnp.where(i2 == e, w2, 0.0)) * inv for e in range(N_EXPERTS)]
    return gates, best


def _plan(best):
    r_io = lax.broadcasted_iota(jnp.int32, (LANES, LANES), 0)
    c_io = lax.broadcasted_iota(jnp.int32, (LANES, LANES), 1)
    before = jnp.where(r_io < c_io, 1.0, 0.0).astype(BF16)
    ranks, counts = [], []
    for g in range(N_GROUPS):
        member = jnp.where(best == g, 1.0, 0.0)
        within = _dot(member.astype(BF16), before)
        tot = jnp.sum(member, axis=1, keepdims=True)
        run = jnp.zeros((1, 1), F32)
        offs = []
        for c in range(N_SLAB):
            offs.append(run)
            run = run + tot[c:c + 1]
        ranks.append((member, within + jnp.concatenate(offs, axis=0)))
        counts.append(run)
    subs = [jnp.floor((n + (SUB - 1.0)) * (1.0 / SUB)) for n in counts]
    start = jnp.zeros((1, 1), F32)
    starts = []
    pos = jnp.zeros(best.shape, F32)
    for g in range(N_GROUPS):
        starts.append(start)
        member, rank = ranks[g]
        pos = pos + member * (start * SUB + rank)
        start = start + subs[g]
    return pos.astype(jnp.int32), jnp.concatenate(starts + subs, axis=0)


def _outproj_chunk(rows, x_refs, yab_ref, yc_ref, w_ref, xo_ref, h_ref, mod, g_norm, wr_split):
    g1, sh2, sc2 = mod
    y = _dot(yab_ref[rows, :], w_ref[0, 0:D_A + D_B, :]) + _dot(yc_ref[rows, :], w_ref[0, D_A + D_B:, :])
    yield
    x = _stream_rows(x_refs, pl.program_id(0), rows) + g1 * y
    xo_ref[rows, :] = x
    h = _rms(x) * g_norm * (1.0 + sc2) + sh2
    h_hi = h.astype(BF16)
    h_ref[rows, :] = h_hi
    h_lo = (h - h_hi.astype(F32)).astype(BF16)
    yield
    prod = _dot(jnp.concatenate([h_hi, h_lo], axis=0), wr_split)
    logits = (prod[0:OUT_ROWS, 0:ROUTER_LANES] + prod[0:OUT_ROWS, ROUTER_LANES:]
              + prod[OUT_ROWS:, 0:ROUTER_LANES])
    yield
    return logits.T


def _outproj_kernel(*refs):
    (yab_ref, yc_ref, mod_ref, g_ref, w_ref, wr_ref, br_ref, xo_ref, h_ref, gates_ref, pos_ref,
     tab_ref) = refs[-12:]
    x_refs = refs[:-12]
    m = mod_ref[0, 0]
    mod = (m[:, 2 * D_MODEL:3 * D_MODEL], m[:, 3 * D_MODEL:4 * D_MODEL], m[:, 4 * D_MODEL:5 * D_MODEL])
    wr = wr_ref[...]
    wr_hi = wr.astype(BF16)
    wr_lo = (wr - wr_hi.astype(F32)).astype(BF16)
    wr_split = jnp.concatenate([wr_hi, wr_lo], axis=1)
    chunks = [_outproj_chunk(pl.ds(c * OUT_ROWS, OUT_ROWS), x_refs, yab_ref, yc_ref, w_ref, xo_ref, h_ref, mod,
                             g_ref[0], wr_split) for c in range(TILE // OUT_ROWS)]
    logits_t = _interleave(chunks)
    lg = [jnp.concatenate([lt[e:e + 1, k * LANES:(k + 1) * LANES] for lt in logits_t for k in range(OUT_ROWS // LANES)],
                          axis=0) for e in range(N_EXPERTS)]
    gates, best = _route(lg, br_ref)
    for e in range(N_EXPERTS):
        gates_ref[0, e] = gates[e]
    pos, tab = _plan(best)
    pos_ref[0] = pos
    tab_ref[0] = jnp.broadcast_to(tab, (SUBLANES, LANES)).astype(jnp.int32)


def _outproj(x, y_ab, y_c, mod, norm_g, w_out_bf, w_router_pad, b_router, layer):
    n_blk = ROWS // TILE
    x_specs, x_args = _stream_specs(x, lambda i: i)
    return pl.pallas_call(
        _outproj_kernel,
        grid=(n_blk,),
        in_specs=x_specs + [
            pl.BlockSpec((TILE, D_A + D_B), lambda i: (i, 0)),
            pl.BlockSpec((TILE, D_C), lambda i: (i, 0)),
            _mod_spec(layer, lambda i: i),
            _layer_vec_spec(layer, D_MODEL),
            pl.BlockSpec((1, D_MODEL, D_MODEL), lambda i: (layer, 0, 0)),
            pl.BlockSpec((D_MODEL, ROUTER_LANES), lambda i: (0, 0)),
            pl.BlockSpec((N_EXPERTS, 1), lambda i: (0, 0)),
        ],
        out_specs=[
            pl.BlockSpec((TILE, D_MODEL), lambda i: (i, 0)),
            pl.BlockSpec((TILE, D_MODEL), lambda i: (i, 0)),
            pl.BlockSpec((1, N_EXPERTS, N_SLAB, LANES), lambda i: (i, 0, 0, 0)),
            pl.BlockSpec((1, N_SLAB, LANES), lambda i: (i, 0, 0)),
            pl.BlockSpec((1, SUBLANES, LANES), lambda i: (i, 0, 0)),
        ],
        out_shape=[
            jax.ShapeDtypeStruct((ROWS, D_MODEL), F32),
            jax.ShapeDtypeStruct((ROWS, D_MODEL), BF16),
            jax.ShapeDtypeStruct((n_blk, N_EXPERTS, N_SLAB, LANES), F32),
            jax.ShapeDtypeStruct((n_blk, N_SLAB, LANES), jnp.int32),
            jax.ShapeDtypeStruct((n_blk, SUBLANES, LANES), jnp.int32),
        ],
        compiler_params=_params("arbitrary"),
        name="outproj_router",
    )(*x_args, y_ab, y_c, mod, norm_g, w_out_bf, w_router_pad, b_router)


MOE_VMEM_LIMIT = 58 * 1024 * 1024


def _moe_kernel(tab_ref, x_ref, h_ref, pos_ref, gates_ref, mod_ref, wg_ref, wu_ref, wd_ref, fg_ref, *rest, final):
    blk = pl.program_id(0)
    grp = pl.program_id(1)
    if final:
        (out_ref,) = rest
    else:
        ng_ref, nu_ref, nd_ref, out_ref, ng_out, nu_out, nd_out = rest
        ng_out[0] = ng_ref[0, 0].astype(BF16)
        nu_out[0] = nu_ref[0, 0].astype(BF16)
        nd_out[0] = nd_ref[0, 0].astype(BF16)

    @pl.when(grp == 0)
    def _():
        out_ref[...] = jnp.zeros_like(out_ref)

    first = tab_ref[blk, grp]
    count = tab_ref[blk, N_GROUPS + grp]
    gates = gates_ref[0, 0]
    g_hi = gates.astype(BF16)
    g_lo = (gates - g_hi.astype(F32)).astype(BF16)
    g_split = jnp.concatenate([g_hi, g_lo], axis=0)

    def sorted_rows(s, n_sub):
        rows = n_sub * SUB
        r_io = lax.broadcasted_iota(jnp.int32, (rows, TILE), 0)
        sel = jnp.where(r_io == pos_ref[...] - (first + s) * SUB, 1.0, 0.0).astype(BF16)
        xs = _dot(sel, h_ref[...]).astype(BF16)
        gs = _dot_nt(sel, g_split)
        y = jnp.zeros((rows, D_MODEL), F32)
        for e in range(EXPERTS_PER_GROUP):
            gate = gs[:, e:e + 1] + gs[:, EXPERTS_PER_GROUP + e:EXPERTS_PER_GROUP + e + 1]
            act = _silu(_dot(xs, wg_ref[e])) * _dot(xs, wu_ref[e]) * gate
            y = y + _dot(act.astype(BF16), wd_ref[e])
        out_ref[...] += _dot_tn(sel, y.astype(BF16))

    triple = (count % 2 == 1) & (count >= 3)
    pairs = (count - jnp.where(triple, 3, 0)) // 2

    def pair(p, carry):
        sorted_rows(2 * p, 2)
        return carry

    lax.fori_loop(0, pairs, pair, 0)

    @pl.when(triple)
    def _():
        sorted_rows(count - 3, 3)

    @pl.when(count == 1)
    def _():
        sorted_rows(0, 1)

    @pl.when(grp == N_GROUPS - 1)
    def _():
        g2 = mod_ref[0, 0][:, 5 * D_MODEL:6 * D_MODEL]
        x = x_ref[...] + g2 * out_ref[...]
        if final:
            x = _rms(x) * fg_ref[...]
        out_ref[...] = x


def _moe(x, h, pos, tab, gates, mod, w_bf, w_next, final_g, layer):
    final = w_next is None
    n_blk = ROWS // TILE
    steps = n_blk * N_GROUPS
    halves = steps // N_EXPERTS
    in_specs = [
        pl.BlockSpec((TILE, D_MODEL), lambda i, g, t: (i, 0)),
        pl.BlockSpec((TILE, D_MODEL), lambda i, g, t: (i, 0)),
        pl.BlockSpec((1, TILE), lambda i, g, t: (0, i)),
        pl.BlockSpec((1, 1, EXPERTS_PER_GROUP, TILE), lambda i, g, t: (i, g, 0, 0)),
        pl.BlockSpec((1, 1, 1, N_MOD * D_MODEL), lambda i, g, t: (layer, _cond_of_tile(i), 0, 0)),
        pl.BlockSpec((EXPERTS_PER_GROUP, D_MODEL, D_EXPERT), lambda i, g, t: (g, 0, 0)),
        pl.BlockSpec((EXPERTS_PER_GROUP, D_MODEL, D_EXPERT), lambda i, g, t: (g, 0, 0)),
        pl.BlockSpec((EXPERTS_PER_GROUP, D_EXPERT, D_MODEL), lambda i, g, t: (g, 0, 0)),
        pl.BlockSpec((1, D_MODEL), lambda i, g, t: (0, 0)),
    ]
    out_specs = [pl.BlockSpec((TILE, D_MODEL), lambda i, g, t: (i, 0))]
    out_shape = [jax.ShapeDtypeStruct((ROWS, D_MODEL), F32)]
    args = [tab, x, h, pos, gates, mod, *w_bf, final_g]
    if not final:
        def slab(i, g):
            step = i * N_GROUPS + g
            return step // halves, step % halves

        for w in w_next:
            rows = w.shape[2] // halves
            in_specs.append(pl.BlockSpec((1, 1, rows, w.shape[3]),
                                         lambda i, g, t: (layer + 1, *slab(i, g), 0)))
            out_specs.append(pl.BlockSpec((1, rows, w.shape[3]), lambda i, g, t: (*slab(i, g), 0)))
            out_shape.append(jax.ShapeDtypeStruct(w.shape[1:], BF16))
            args.append(w)
    res = pl.pallas_call(
        functools.partial(_moe_kernel, final=final),
        grid_spec=pltpu.PrefetchScalarGridSpec(num_scalar_prefetch=1, grid=(n_blk, N_GROUPS),
                                               in_specs=in_specs, out_specs=out_specs),
        out_shape=out_shape,
        compiler_params=pltpu.CompilerParams(dimension_semantics=("arbitrary", "arbitrary"),
                                             vmem_limit_bytes=MOE_VMEM_LIMIT),
        name="moe",
    )(*args)
    return res[0], (None if final else tuple(res[1:]))


def kernel(x_prompt, x_sample, state_hgrn, c, c_ctx, norm_mix_g, norm_ffn_g, final_norm_g, w_ada, b_ada, w_in, w_out, conv_a_w, conv_b_w, conv_b_b, ln_b_g, ln_b_b, lb_logits, gnorm_c_g, w_router, b_router, w_gate, w_up, w_down):
    batch, seq, _ = x_prompt.shape
    dec_batch, dec_seq, _ = x_sample.shape
    assert batch * seq == STREAM_ROWS and dec_batch * dec_seq == STREAM_ROWS
    assert dec_seq == TILE and TILE % seq == 0 and dec_batch + 1 <= 8

    vec = lambda a: a.reshape(DEPTH, 1, a.shape[-1])
    w_out_bf = w_out.astype(BF16)
    w_experts = (w_gate, w_up, w_down)
    w_bf = tuple(w[0].astype(BF16) for w in w_experts)
    w_router_pad = jnp.pad(w_router, ((0, 0), (0, ROUTER_LANES - N_EXPERTS)))
    b_router2 = b_router.reshape(N_EXPERTS, 1)
    final_g = final_norm_g.reshape(1, D_MODEL)
    norm_mix, norm_ffn, gn_g = vec(norm_mix_g), vec(norm_ffn_g), vec(gnorm_c_g)
    cb_b, ln_g, ln_b = vec(conv_b_b), vec(ln_b_g), vec(ln_b_b)

    lower_bound = _lower_bound(lb_logits).reshape(2, DEPTH, 1, D_C)
    cond8 = jnp.concatenate([c_ctx[None, :], c, jnp.zeros((8 - 1 - dec_batch, D_MODEL), F32)], axis=0)
    mod = _adaln(cond8, w_ada, b_ada).reshape(DEPTH, 8, 1, N_MOD * D_MODEL)

    x = (x_prompt.reshape(STREAM_ROWS, D_MODEL), x_sample.reshape(STREAM_ROWS, D_MODEL))
    states = None
    for l in range(DEPTH):
        y_ab, h_mix = _conv_mix(x, mod, norm_mix, w_in, conv_a_w, conv_b_w, cb_b, ln_g, ln_b, l, seq)
        y_c, states = _hgrn_mix(h_mix, w_in, lower_bound, gn_g, state_hgrn, states, l, TILE // seq)
        x, h, gates, pos, tab = _outproj(x, y_ab, y_c, mod, norm_ffn, w_out_bf, w_router_pad, b_router2, l)
        x, w_bf = _moe(x, h, pos.reshape(1, ROWS), tab[:, :, 0],
                       gates.reshape(ROWS // TILE, N_GROUPS, EXPERTS_PER_GROUP, TILE), mod,
                       w_bf, w_experts if l + 1 < DEPTH else None, final_g, l)
    return (x[:STREAM_ROWS].reshape(batch, seq, D_MODEL), x[STREAM_ROWS:].reshape(dec_batch, dec_seq, D_MODEL),
            states)
```
